```python
import jax, jax.numpy as jnp
from jax import lax
import numpy as np

D_MODEL = 1024
BATCH = 16
SEQ = 2048
DEPTH = 2

N_A_LAYERS = DEPTH // 2
N_B_LAYERS = DEPTH - N_A_LAYERS
LRU_WIDTH = D_MODEL
LRU_HEADS = 8
LRU_BLOCK = LRU_WIDTH // LRU_HEADS
CONV_WIDTH = 4
LRU_C = 8.0
ATTN_HEADS = 16
ATTN_HEAD_DIM = 64
ATTN_WIDTH = ATTN_HEADS * ATTN_HEAD_DIM
Q_BLOCK = 128
PEER_HEADS = 8
PEER_NKEYS = 128
PEER_EXPERTS = PEER_NKEYS * PEER_NKEYS
PEER_TOPK = 16
PEER_QDIM = 256
PEER_HALF = PEER_QDIM // 2
PEER_CHUNK = 128
PLE_DIM = 256
EPS = 1e-6

kernel_name = 'yoco_rglru_stickbreak_peer'


def rms_norm(x, g):
    xf = x.astype(jnp.float32)
    y = xf * lax.rsqrt(jnp.mean(xf * xf, axis=-1, keepdims=True) + EPS)
    return (y * g.astype(jnp.float32)).astype(x.dtype)


def rglru_block(h, w_in, conv_w, conv_b, w_r, w_i, b_r, b_i, lam, w_out):
    B, S, _ = h.shape
    proj = h @ w_in
    y_branch, x_branch = jnp.split(proj, 2, axis=-1)
    xc = lax.conv_general_dilated(x_branch, conv_w[:, None, :], window_strides=(1,),
                                  padding=((CONV_WIDTH - 1, 0),),
                                  dimension_numbers=('NWC', 'WIO', 'NWC'),
                                  feature_group_count=LRU_WIDTH) + conv_b
    xb = xc.reshape(B, S, LRU_HEADS, LRU_BLOCK)
    r = jax.nn.sigmoid(jnp.einsum('bshi,hij->bshj', xb, w_r).reshape(B, S, LRU_WIDTH).astype(jnp.float32)
                       + b_r.astype(jnp.float32))
    ig = jax.nn.sigmoid(jnp.einsum('bshi,hij->bshj', xb, w_i).reshape(B, S, LRU_WIDTH).astype(jnp.float32)
                        + b_i.astype(jnp.float32))
    log_a = -LRU_C * r * jax.nn.softplus(-lam.astype(jnp.float32))
    a = jnp.exp(log_a)
    u = jnp.sqrt(-jnp.expm1(2.0 * log_a)) * ig * xc.astype(jnp.float32)

    def combine(left, right):
        a1, b1 = left
        a2, b2 = right
        return a1 * a2, a2 * b1 + b2

    _, hseq = lax.associative_scan(combine, (a, u), axis=1)
    gated = (jax.nn.gelu(y_branch.astype(jnp.float32)) * hseq).astype(h.dtype)
    return gated @ w_out


def stick_breaking_attention(q, k, v):
    S = q.shape[1]
    scale = ATTN_HEAD_DIM ** -0.5
    outs = []
    for blk in range(S // Q_BLOCK):
        q0 = blk * Q_BLOCK
        end = q0 + Q_BLOCK
        qb = q[:, q0:end].astype(jnp.float32)
        kb = k[:, :end].astype(jnp.float32)
        z = jnp.einsum('bqhd,bkhd->bhqk', qb, kb) * scale
        t_pos = q0 + jnp.arange(Q_BLOCK)[:, None]
        s_pos = jnp.arange(end)[None, :]
        causal = s_pos < t_pos
        lbeta = jax.nn.log_sigmoid(z)
        l1m = jnp.where(causal, lbeta - z, 0.0)
        rc = lax.cumsum(l1m, axis=3, reverse=True)
        A = jnp.where(causal, jnp.exp(lbeta + rc - l1m), 0.0)
        outs.append(jnp.einsum('bhqk,bkhd->bqhd', A, v[:, :end].astype(jnp.float32)))
    return jnp.concatenate(outs, axis=1).astype(v.dtype)


def peer_ffn(h, w_q, sub_keys, u_tab, v_tab):
    B, S, D = h.shape
    T = B * S
    xt = h.reshape(T, D)
    q = (xt @ w_q).reshape(T, PEER_HEADS, 2, PEER_HALF).astype(jnp.float32)
    s = jnp.einsum('thpd,hpnd->thpn', q, sub_keys.astype(jnp.float32))
    sv, si = lax.top_k(s, PEER_TOPK)
    cand = sv[:, :, 0, :, None] + sv[:, :, 1, None, :]
    cv, ci = lax.top_k(cand.reshape(T, PEER_HEADS, PEER_TOPK * PEER_TOPK), PEER_TOPK)
    i1 = jnp.take_along_axis(si[:, :, 0], ci // PEER_TOPK, axis=-1)
    i2 = jnp.take_along_axis(si[:, :, 1], ci % PEER_TOPK, axis=-1)
    eidx = (i1 * PEER_NKEYS + i2).reshape(T, PEER_HEADS * PEER_TOPK)
    gates = jax.nn.softmax(cv, axis=-1).reshape(T, PEER_HEADS * PEER_TOPK)
    n_chunks = T // PEER_CHUNK

    def chunk_fn(args):
        xc, ic, gc = args
        uc = jnp.take(u_tab, ic, axis=0)
        vc = jnp.take(v_tab, ic, axis=0)
        act = jax.nn.gelu(jnp.einsum('cd,ckd->ck', xc, uc).astype(jnp.float32))
        return jnp.einsum('ck,ckd->cd', (gc * act).astype(vc.dtype), vc)

    out = lax.map(chunk_fn, (xt.reshape(n_chunks, PEER_CHUNK, D),
                             eidx.reshape(n_chunks, PEER_CHUNK, PEER_HEADS * PEER_TOPK),
                             gates.reshape(n_chunks, PEER_CHUNK, PEER_HEADS * PEER_TOPK)))
    return out.reshape(B, S, D).astype(h.dtype)


def per_layer_embed(h, p_i, g, w_gate, w_proj):
    gate = jax.nn.sigmoid((rms_norm(h, g) @ w_gate).astype(jnp.float32))
    return h + (gate * (p_i @ w_proj).astype(jnp.float32)).astype(h.dtype)


def setup_inputs(seed: int = 0) -> dict:
    key = jax.random.key(seed)
    ks = jax.random.split(key, 28)
    nrm = jax.random.normal
    f32 = jnp.float32
    a8 = jax.random.uniform(ks[9], (N_A_LAYERS, LRU_WIDTH), f32, 0.9, 0.999)
    a0 = a8 ** (1.0 / LRU_C)
    return {
        'x': nrm(ks[0], (BATCH, SEQ, D_MODEL), f32),
        'p': nrm(ks[1], (DEPTH, BATCH, SEQ, PLE_DIM), f32),
        'norm_mix': 1.0 + 0.02 * nrm(ks[2], (DEPTH, D_MODEL), f32),
        'a_w_in': nrm(ks[3], (N_A_LAYERS, D_MODEL, 2 * LRU_WIDTH), f32) * D_MODEL ** -0.5,
        'a_conv_w': nrm(ks[4], (N_A_LAYERS, CONV_WIDTH, LRU_WIDTH), f32) * CONV_WIDTH ** -0.5,
        'a_conv_b': 0.02 * nrm(ks[5], (N_A_LAYERS, LRU_WIDTH), f32),
        'a_w_r': nrm(ks[6], (N_A_LAYERS, LRU_HEADS, LRU_BLOCK, LRU_BLOCK), f32) * LRU_BLOCK ** -0.5,
        'a_w_i': nrm(ks[7], (N_A_LAYERS, LRU_HEADS, LRU_BLOCK, LRU_BLOCK), f32) * LRU_BLOCK ** -0.5,
        'a_b_r': 0.02 * nrm(ks[8], (N_A_LAYERS, LRU_WIDTH), f32),
        'a_b_i': 0.02 * nrm(ks[10], (N_A_LAYERS, LRU_WIDTH), f32),
        'a_lambda': jnp.log(a0) - jnp.log1p(-a0),
        'a_w_out': nrm(ks[11], (N_A_LAYERS, LRU_WIDTH, D_MODEL), f32) * LRU_WIDTH ** -0.5,
        'kv_norm': 1.0 + 0.02 * nrm(ks[12], (D_MODEL,), f32),
        'w_kv': nrm(ks[13], (D_MODEL, 2 * ATTN_WIDTH), f32) * D_MODEL ** -0.5,
        'b_w_q': nrm(ks[14], (N_B_LAYERS, D_MODEL, ATTN_WIDTH), f32) * D_MODEL ** -0.5,
        'b_w_o': nrm(ks[15], (N_B_LAYERS, ATTN_WIDTH, D_MODEL), f32) * ATTN_WIDTH ** -0.5,
        'norm_ffn': 1.0 + 0.02 * nrm(ks[16], (DEPTH, D_MODEL), f32),
        'peer_w_q': nrm(ks[17], (DEPTH, D_MODEL, PEER_HEADS * PEER_QDIM), f32) * D_MODEL ** -0.5,
        'peer_sub_keys': nrm(ks[18], (DEPTH, PEER_HEADS, 2, PEER_NKEYS, PEER_HALF), f32) * PEER_HALF ** -0.5,
        'peer_u': nrm(ks[19], (DEPTH, PEER_EXPERTS, D_MODEL), f32) * D_MODEL ** -0.5,
        'peer_v': nrm(ks[20], (DEPTH, PEER_EXPERTS, D_MODEL), f32) * (PEER_HEADS * PEER_TOPK) ** -0.5,
        'norm_ple': 1.0 + 0.02 * nrm(ks[21], (DEPTH, D_MODEL), f32),
        'ple_w_gate': nrm(ks[22], (DEPTH, D_MODEL, D_MODEL), f32) * D_MODEL ** -0.5,
        'ple_w_proj': nrm(ks[23], (DEPTH, PLE_DIM, D_MODEL), f32) * PLE_DIM ** -0.5,
        'final_norm': 1.0 + 0.02 * nrm(ks[24], (D_MODEL,), f32),
    }


def reference(x, p, norm_mix, a_w_in, a_conv_w, a_conv_b, a_w_r, a_w_i, a_b_r, a_b_i, a_lambda,
              a_w_out, kv_norm, w_kv, b_w_q, b_w_o, norm_ffn, peer_w_q, peer_sub_keys, peer_u,
              peer_v, norm_ple, ple_w_gate, ple_w_proj, final_norm):
    B, S, _ = x.shape
    h = x
    k_shared = None
    v_shared = None
    for i in range(DEPTH):
        hn = rms_norm(h, norm_mix[i])
        if i < N_A_LAYERS:
            j = i
            h = h + rglru_block(hn, a_w_in[j], a_conv_w[j], a_conv_b[j], a_w_r[j], a_w_i[j],
                                a_b_r[j], a_b_i[j], a_lambda[j], a_w_out[j])
        else:
            j = i - N_A_LAYERS
            if i == N_A_LAYERS:
                kvp = rms_norm(h, kv_norm) @ w_kv
                k_flat, v_flat = jnp.split(kvp, 2, axis=-1)
                k_shared = k_flat.reshape(B, S, ATTN_HEADS, ATTN_HEAD_DIM)
                v_shared = v_flat.reshape(B, S, ATTN_HEADS, ATTN_HEAD_DIM)
            q = (hn @ b_w_q[j]).reshape(B, S, ATTN_HEADS, ATTN_HEAD_DIM)
            o = stick_breaking_attention(q, k_shared, v_shared).reshape(B, S, ATTN_WIDTH)
            h = h + o @ b_w_o[j]
        h = h + peer_ffn(rms_norm(h, norm_ffn[i]), peer_w_q[i], peer_sub_keys[i], peer_u[i], peer_v[i])
        h = per_layer_embed(h, p[i], norm_ple[i], ple_w_gate[i], ple_w_proj[i])
    return rms_norm(h, final_norm)
```

```python
import functools

import jax
import jax.numpy as jnp
from jax import lax
from jax.experimental import pallas as pl
from jax.experimental.pallas import tpu as pltpu

F32 = jnp.float32
BF16 = jnp.bfloat16

EPS = 1e-6
LRU_HEADS = 8
CONV_WIDTH = 4
LRU_C = 8.0
ATTN_HEADS = 16
ATTN_HEAD_DIM = 64
Q_BLOCK = 128
PEER_HEADS = 8
PEER_NKEYS = 128
PEER_TOPK = 16
PEER_CHUNK = 128

VMEM_LIMIT = 48 * 1024 * 1024


def _rms(x, g):
    return x * lax.rsqrt(jnp.mean(x * x, axis=-1, keepdims=True) + EPS) * g


def _norm_matmul_body(x_ref, g_ref, w_ref, o_ref):
    xn = _rms(x_ref[...], g_ref[...]).astype(BF16)
    o_ref[...] = jnp.dot(xn, w_ref[...], preferred_element_type=F32).astype(o_ref.dtype)


def norm_matmul(x, g, w, *, tm=512, out_dtype=F32):
    T, D = x.shape
    N = w.shape[1]
    return pl.pallas_call(
        _norm_matmul_body,
        grid=(T // tm,),
        in_specs=[
            pl.BlockSpec((tm, D), lambda i: (i, 0)),
            pl.BlockSpec((1, D), lambda i: (0, 0)),
            pl.BlockSpec((D, N), lambda i: (0, 0)),
        ],
        out_specs=pl.BlockSpec((tm, N), lambda i: (i, 0)),
        out_shape=jax.ShapeDtypeStruct((T, N), out_dtype),
        compiler_params=pltpu.CompilerParams(
            dimension_semantics=("parallel",), vmem_limit_bytes=VMEM_LIMIT),
        name="norm_matmul",
    )(x, g.reshape(1, D), w.astype(BF16))


def _rglru_rest(proj, conv_w, conv_b, w_r, w_i, b_r, b_i, lam, w_out, B, S):
    W = proj.shape[-1] // 2
    proj = proj.reshape(B, S, 2 * W)
    y_branch, x_branch = proj[..., :W], proj[..., W:]
    xp = jnp.pad(x_branch, ((0, 0), (CONV_WIDTH - 1, 0), (0, 0)))
    xc = sum(conv_w[k] * xp[:, k:k + S] for k in range(CONV_WIDTH)) + conv_b
    xb = xc.reshape(B, S, LRU_HEADS, W // LRU_HEADS)
    r = jax.nn.sigmoid(jnp.einsum('bshi,hij->bshj', xb, w_r).reshape(B, S, W) + b_r)
    ig = jax.nn.sigmoid(jnp.einsum('bshi,hij->bshj', xb, w_i).reshape(B, S, W) + b_i)
    log_a = -LRU_C * r * jax.nn.softplus(-lam)
    a = jnp.exp(log_a)
    u = jnp.sqrt(-jnp.expm1(2.0 * log_a)) * ig * xc

    def combine(left, right):
        a1, b1 = left
        a2, b2 = right
        return a1 * a2, a2 * b1 + b2

    _, hseq = lax.associative_scan(combine, (a, u), axis=1)
    gated = jax.nn.gelu(y_branch) * hseq
    return (gated.reshape(B * S, W) @ w_out)


def _stick_breaking(q, k, v):
    S = q.shape[1]
    scale = ATTN_HEAD_DIM ** -0.5
    outs = []
    for blk in range(S // Q_BLOCK):
        q0 = blk * Q_BLOCK
        end = q0 + Q_BLOCK
        z = jnp.einsum('bqhd,bkhd->bhqk', q[:, q0:end], k[:, :end]) * scale
        t_pos = q0 + jnp.arange(Q_BLOCK)[:, None]
        s_pos = jnp.arange(end)[None, :]
        causal = s_pos < t_pos
        lbeta = jax.nn.log_sigmoid(z)
        l1m = jnp.where(causal, lbeta - z, 0.0)
        rc = lax.cumsum(l1m, axis=3, reverse=True)
        A = jnp.where(causal, jnp.exp(lbeta + rc - l1m), 0.0)
        outs.append(jnp.einsum('bhqk,bkhd->bqhd', A, v[:, :end]))
    return jnp.concatenate(outs, axis=1)


def _peer_rest(xt, q, sub_keys, u_tab, v_tab):
    T, D = xt.shape
    half = sub_keys.shape[-1]
    q = q.reshape(T, PEER_HEADS, 2, half)
    s = jnp.einsum('thpd,hpnd->thpn', q, sub_keys)
    sv, si = lax.top_k(s, PEER_TOPK)
    cand = sv[:, :, 0, :, None] + sv[:, :, 1, None, :]
    cv, ci = lax.top_k(cand.reshape(T, PEER_HEADS, PEER_TOPK * PEER_TOPK), PEER_TOPK)
    i1 = jnp.take_along_axis(si[:, :, 0], ci // PEER_TOPK, axis=-1)
    i2 = jnp.take_along_axis(si[:, :, 1], ci % PEER_TOPK, axis=-1)
    eidx = (i1 * PEER_NKEYS + i2).reshape(T, PEER_HEADS * PEER_TOPK)
    gates = jax.nn.softmax(cv, axis=-1).reshape(T, PEER_HEADS * PEER_TOPK)
    n_chunks = T // PEER_CHUNK

    def chunk_fn(args):
        xc, ic, gc = args
        uc = jnp.take(u_tab, ic, axis=0)
        vc = jnp.take(v_tab, ic, axis=0)
        act = jax.nn.gelu(jnp.einsum('cd,ckd->ck', xc, uc))
        return jnp.einsum('ck,ckd->cd', gc * act, vc)

    out = lax.map(chunk_fn, (xt.reshape(n_chunks, PEER_CHUNK, D),
                             eidx.reshape(n_chunks, PEER_CHUNK, -1),
                             gates.reshape(n_chunks, PEER_CHUNK, -1)))
    return out.reshape(T, D)


def kernel(x, p, norm_mix, a_w_in, a_conv_w, a_conv_b, a_w_r, a_w_i, a_b_r, a_b_i, a_lambda,
           a_w_out, kv_norm, w_kv, b_w_q, b_w_o, norm_ffn, peer_w_q, peer_sub_keys, peer_u,
           peer_v, norm_ple, ple_w_gate, ple_w_proj, final_norm):
    B, S, D = x.shape
    T = B * S
    depth = norm_mix.shape[0]
    n_a = a_w_in.shape[0]
    h = x.reshape(T, D)
    k_sh = v_sh = None
    for i in range(depth):
        if i < n_a:
            j = i
            proj = norm_matmul(h, norm_mix[i], a_w_in[j])
            h = h + _rglru_rest(proj, a_conv_w[j], a_conv_b[j], a_w_r[j], a_w_i[j], a_b_r[j],
                                a_b_i[j], a_lambda[j], a_w_out[j], B, S)
        else:
            j = i - n_a
            if i == n_a:
                kvp = norm_matmul(h, kv_norm, w_kv)
                aw = kvp.shape[-1] // 2
                k_sh = kvp[:, :aw].reshape(B, S, ATTN_HEADS, ATTN_HEAD_DIM)
                v_sh = kvp[:, aw:].reshape(B, S, ATTN_HEADS, ATTN_HEAD_DIM)
            q = norm_matmul(h, norm_mix[i], b_w_q[j]).reshape(B, S, ATTN_HEADS, ATTN_HEAD_DIM)
            o = _stick_breaking(q, k_sh, v_sh).reshape(T, -1)
            h = h + o @ b_w_o[j]
        xt = _rms(h, norm_ffn[i])
        pq = norm_matmul(h, norm_ffn[i], peer_w_q[i])
        h = h + _peer_rest(xt, pq, peer_sub_keys[i], peer_u[i], peer_v[i])
        gate = jax.nn.sigmoid(norm_matmul(h, norm_ple[i], ple_w_gate[i]))
        h = h + gate * (p[i].reshape(T, -1) @ ple_w_proj[i])
    return _rms(h, final_norm).reshape(B, S, D)
```

```python
import functools

import jax
import jax.numpy as jnp
from jax import lax
from jax.experimental import pallas as pl
from jax.experimental.pallas import tpu as pltpu

F32 = jnp.float32
BF16 = jnp.bfloat16

EPS = 1e-6
LRU_HEADS = 8
CONV_WIDTH = 4
LRU_C = 8.0
ATTN_HEADS = 16
ATTN_HEAD_DIM = 64
PEER_HEADS = 8
PEER_NKEYS = 128
PEER_TOPK = 16

SUB, LANE = 8, 128
VMEM_LIMIT = 48 * 1024 * 1024


def _params(*semantics):
    return pltpu.CompilerParams(dimension_semantics=semantics, vmem_limit_bytes=VMEM_LIMIT)


def _rms(x, g):
    return x * lax.rsqrt(jnp.mean(x * x, axis=-1, keepdims=True) + EPS) * g


def _softplus(z):
    return jnp.maximum(z, 0.0) + jnp.log1p(jnp.exp(-jnp.abs(z)))


def _resident(shape):
    return pl.BlockSpec(shape, lambda *_: (0,) * len(shape))


def _rglru_body(h_ref, g_ref, win_ref, cw_ref, cb_ref, wr_ref, wi_ref, br_ref, bi_ref, lam_ref,
                wout_ref, o_ref, tail_ref, state_ref):
    ts, _ = h_ref.shape
    width = wout_ref.shape[0]
    blk = width // LRU_HEADS

    @pl.when(pl.program_id(1) == 0)
    def _():
        tail_ref[...] = jnp.zeros_like(tail_ref)
        state_ref[...] = jnp.zeros_like(state_ref)

    x = h_ref[...]
    proj = jnp.dot(_rms(x, g_ref[...]).astype(BF16), win_ref[...], preferred_element_type=F32)
    yb, xb = proj[:, :width], proj[:, width:]

    row = lax.broadcasted_iota(jnp.int32, (ts, width), 0)
    ext = jnp.concatenate([tail_ref[...], xb], axis=0)
    xc = cw_ref[CONV_WIDTH - 1:CONV_WIDTH, :] * xb + cb_ref[...]
    for d in range(1, CONV_WIDTH):
        xc = xc + cw_ref[CONV_WIDTH - 1 - d:CONV_WIDTH - d, :] * pltpu.roll(ext, d, axis=0)[SUB:, :]
    tail_ref[...] = xb[ts - SUB:, :]

    xcb = xc.astype(BF16)
    r = jnp.concatenate([jnp.dot(xcb[:, k * blk:(k + 1) * blk], wr_ref[k], preferred_element_type=F32)
                         for k in range(LRU_HEADS)], axis=1)
    ig = jnp.concatenate([jnp.dot(xcb[:, k * blk:(k + 1) * blk], wi_ref[k], preferred_element_type=F32)
                          for k in range(LRU_HEADS)], axis=1)
    r = jax.nn.sigmoid(r + br_ref[...])
    ig = jax.nn.sigmoid(ig + bi_ref[...])
    log_a = -LRU_C * r * _softplus(-lam_ref[...])
    a = jnp.exp(log_a)
    u = jnp.sqrt(-jnp.tanh(log_a) * (a * a + 1.0)) * ig * xc

    d = 1
    while d < ts:
        keep = row >= d
        a_prev = jnp.where(keep, pltpu.roll(a, d, axis=0), 1.0)
        u_prev = jnp.where(keep, pltpu.roll(u, d, axis=0), 0.0)
        u = a * u_prev + u
        a = a * a_prev
        d *= 2
    hseq = u + a * state_ref[SUB - 1:SUB, :]
    state_ref[...] = hseq[ts - SUB:, :]

    gated = (jax.nn.gelu(yb) * hseq).astype(BF16)
    o_ref[...] = x + jnp.dot(gated, wout_ref[...], preferred_element_type=F32)


def rglru_mixer(h, g, w_in, conv_w, conv_b, w_r, w_i, b_r, b_i, lam, w_out, *, batch, ts=256):
    T, D = h.shape
    S = T // batch
    nt = S // ts
    W = w_out.shape[0]
    tile = pl.BlockSpec((ts, D), lambda b, j: (b * nt + j, 0))
    vec = lambda a: a.reshape(1, -1)
    return pl.pallas_call(
        _rglru_body,
        grid=(batch, nt),
        in_specs=[tile, _resident((1, D)), _resident((D, 2 * W)), _resident((CONV_WIDTH, W)),
                  _resident((1, W)), _resident(w_r.shape), _resident(w_i.shape), _resident((1, W)),
                  _resident((1, W)), _resident((1, W)), _resident((W, D))],
        out_specs=tile,
        out_shape=jax.ShapeDtypeStruct((T, D), F32),
        scratch_shapes=[pltpu.VMEM((SUB, W), F32), pltpu.VMEM((SUB, W), F32)],
        compiler_params=_params("parallel", "arbitrary"),
        name="rglru_mixer",
    )(h, vec(g), w_in.astype(BF16), conv_w, vec(conv_b), w_r.astype(BF16), w_i.astype(BF16),
      vec(b_r), vec(b_i), vec(lam), w_out.astype(BF16))


def _qkv_body(h_ref, gq_ref, gkv_ref, wq_ref, wkv_ref, q_ref, k_ref, v_ref, *, scale):
    x = h_ref[...]
    xhat = x * lax.rsqrt(jnp.mean(x * x, axis=-1, keepdims=True) + EPS)
    q = jnp.dot((xhat * gq_ref[...]).astype(BF16), wq_ref[...], preferred_element_type=F32)
    kv = jnp.dot((xhat * gkv_ref[...]).astype(BF16), wkv_ref[...], preferred_element_type=F32)
    aw = k_ref.shape[-1]
    q_ref[...] = (q * scale).astype(q_ref.dtype)
    k_ref[...] = kv[:, :aw].astype(k_ref.dtype)
    v_ref[...] = kv[:, aw:].astype(v_ref.dtype)


def qkv_project(h, g_q, g_kv, w_q, w_kv, *, tm=512):
    T, D = h.shape
    aw = w_q.shape[1]
    tile = pl.BlockSpec((tm, D), lambda i: (i, 0))
    out = pl.BlockSpec((tm, aw), lambda i: (i, 0))
    return pl.pallas_call(
        functools.partial(_qkv_body, scale=ATTN_HEAD_DIM ** -0.5),
        grid=(T // tm,),
        in_specs=[tile, _resident((1, D)), _resident((1, D)), _resident((D, aw)), _resident((D, 2 * aw))],
        out_specs=[out, out, out],
        out_shape=[jax.ShapeDtypeStruct((T, aw), BF16)] * 3,
        compiler_params=_params("parallel"),
        name="qkv_project",
    )(h, g_q.reshape(1, D), g_kv.reshape(1, D), w_q.astype(BF16), w_kv.astype(BF16))


ATTN_TQ, ATTN_TK = 512, 256


def _stick_body(q_ref, k_ref, v_ref, o_ref, *, tq, tk):
    S, dh = q_ref.shape
    ndiag = tq // tk
    row = lax.broadcasted_iota(jnp.int32, (tk, tk), 0)
    col = lax.broadcasted_iota(jnp.int32, (tk, tk), 1)
    later = (row > col).astype(BF16)
    qpos = lax.broadcasted_iota(jnp.int32, (tq, tk), 0)
    kpos = lax.broadcasted_iota(jnp.int32, (tq, tk), 1)

    def pair(q, kb, carry, shift):
        acc, right = carry
        k = k_ref[pl.ds(pl.multiple_of(kb * tk, tk), tk), :]
        v = v_ref[pl.ds(pl.multiple_of(kb * tk, tk), tk), :]
        z = lax.dot_general(q, k, (((1,), (1,)), ((), ())), preferred_element_type=F32)
        sp = _softplus(z)
        l1m = -sp
        if shift is not None:
            causal = kpos < qpos + shift
            l1m = jnp.where(causal, l1m, 0.0)
        hi = l1m.astype(BF16)
        lo = (l1m - hi.astype(F32)).astype(BF16)
        after = (jnp.dot(hi, later, preferred_element_type=F32)
                 + jnp.dot(lo, later, preferred_element_type=F32))
        a = jnp.exp((z - sp) + after + right)
        if shift is not None:
            a = jnp.where(causal, a, 0.0)
        acc = acc + jnp.dot(a.astype(BF16), v, preferred_element_type=F32)
        right = right + after[:, 0:1] + l1m[:, 0:1]
        return acc, right

    def q_block(qi, c):
        q = q_ref[pl.ds(pl.multiple_of(qi * tq, tq), tq), :]
        carry = (jnp.zeros((tq, dh), F32), jnp.zeros((tq, 1), F32))
        last = (qi + 1) * ndiag - 1
        for d in range(ndiag):
            carry = pair(q, last - d, carry, (d + 1 - ndiag) * tk)
        carry = lax.fori_loop(0, qi * ndiag, lambda n, cr: pair(q, qi * ndiag - 1 - n, cr, None), carry)
        o_ref[pl.ds(pl.multiple_of(qi * tq, tq), tq), :] = carry[0].astype(o_ref.dtype)
        return c

    lax.fori_loop(0, S // tq, q_block, 0)


def stick_breaking_attention(q, k, v, *, tq=ATTN_TQ, tk=ATTN_TK):
    B, H, S, dh = q.shape
    tq = min(tq, S)
    tk = min(tk, tq)
    spec = pl.BlockSpec((None, None, S, dh), lambda b, h: (b, h, 0, 0))
    return pl.pallas_call(
        functools.partial(_stick_body, tq=tq, tk=tk),
        grid=(B, H),
        in_specs=[spec, spec, spec],
        out_specs=spec,
        out_shape=jax.ShapeDtypeStruct((B, H, S, dh), BF16),
        compiler_params=_params("parallel", "parallel"),
        name="stick_breaking",
    )(q, k, v)


def _matmul_residual_body(h_ref, x_ref, w_ref, o_ref):
    o_ref[...] = h_ref[...] + jnp.dot(x_ref[...], w_ref[...], preferred_element_type=F32)


def matmul_residual(h, x, w, *, tm=512):
    T, D = h.shape
    K = x.shape[1]
    return pl.pallas_call(
        _matmul_residual_body,
        grid=(T // tm,),
        in_specs=[pl.BlockSpec((tm, D), lambda i: (i, 0)), pl.BlockSpec((tm, K), lambda i: (i, 0)),
                  _resident((K, D))],
        out_specs=pl.BlockSpec((tm, D), lambda i: (i, 0)),
        out_shape=jax.ShapeDtypeStruct((T, D), F32),
        compiler_params=_params("parallel"),
        name="matmul_residual",
    )(h, x, w.astype(BF16))


def _topk_rows(s, k, payload=None):
    n = s.shape[0]
    rows = lax.broadcasted_iota(jnp.int32, s.shape, 0)
    vals, sel = [], []
    for _ in range(k):
        best = jnp.max(s, axis=0, keepdims=True)
        pos = jnp.min(jnp.where(s == best, rows, n), axis=0, keepdims=True)
        hit = rows == pos
        vals.append(best)
        sel.append(pos if payload is None else jnp.max(jnp.where(hit, payload, -1), axis=0, keepdims=True))
        s = jnp.where(hit, -jnp.inf, s)
    return jnp.concatenate(vals, axis=0), jnp.concatenate(sel, axis=0)


def _peer_route_body(h_ref, g_ref, wqt_ref, keys_ref, eidx_ref, gates_ref):
    xn = _rms(h_ref[...], g_ref[...]).astype(BF16)
    qt = lax.dot_general(wqt_ref[...], xn, (((1,), (1,)), ((), ())), preferred_element_type=F32)
    half = keys_ref.shape[-1]
    k = PEER_TOPK
    for hd in range(PEER_HEADS):
        sv, si = [], []
        for part in range(2):
            hp = hd * 2 + part
            q = qt[hp * half:(hp + 1) * half, :].astype(BF16)
            scores = jnp.dot(keys_ref[hp], q, preferred_element_type=F32)
            v, i = _topk_rows(scores, k)
            sv.append(v)
            si.append(i)
        cand = jnp.concatenate([sv[0][a:a + 1, :] + sv[1] for a in range(k)], axis=0)
        pay = jnp.concatenate([si[0][a:a + 1, :] * PEER_NKEYS + si[1] for a in range(k)], axis=0)
        cv, ce = _topk_rows(cand, k, payload=pay)
        ex = jnp.exp(cv - cv[0:1, :])
        gates_ref[hd * k:(hd + 1) * k, :] = ex / jnp.sum(ex, axis=0, keepdims=True)
        eidx_ref[hd * k:(hd + 1) * k, :] = ce


def peer_route(h, g, w_q, sub_keys, *, tm=256):
    T, D = h.shape
    H, _, nkeys, half = sub_keys.shape
    npick = H * PEER_TOPK
    return pl.pallas_call(
        _peer_route_body,
        grid=(T // tm,),
        in_specs=[pl.BlockSpec((tm, D), lambda i: (i, 0)), _resident((1, D)),
                  _resident((H * 2 * half, D)), _resident((H * 2, nkeys, half))],
        out_specs=[pl.BlockSpec((npick, tm), lambda i: (0, i)),
                   pl.BlockSpec((npick, tm), lambda i: (0, i))],
        out_shape=[jax.ShapeDtypeStruct((npick, T), jnp.int32),
                   jax.ShapeDtypeStruct((npick, T), F32)],
        compiler_params=_params("parallel"),
        name="peer_route",
    )(h, g.reshape(1, D), w_q.T.astype(BF16), sub_keys.reshape(H * 2, nkeys, half).astype(BF16))


NPICK = PEER_HEADS * PEER_TOPK
HIGH_HALF = -65536


def _peer_expert_body(idx_cur, idx_nxt, h_ref, g_ref, gates_ref, tab_ref, o_ref, buf0, buf1, vscr, sem, *,
                      tt, nsteps, unroll):
    i = pl.program_id(0)
    tok_lane = lax.broadcasted_iota(jnp.int32, (NPICK, tt), 1)
    g = g_ref[...]

    def issue_token(idx_ref, t, dst, dsem):
        for k in range(NPICK):
            row = pl.multiple_of((t * NPICK + k) * SUB, SUB)
            pltpu.make_async_copy(tab_ref.at[idx_ref[t, k]], dst.at[pl.ds(row, SUB), :],
                                  dsem).start(priority=k % 2)

    def compute_token(t, src, vs):
        x = h_ref[pl.ds(t, 1), :]
        xn = _rms(x, g)
        base = pl.multiple_of(t * (NPICK * SUB), SUB)
        r = jnp.zeros((NPICK, LANE), F32)
        for s in range(SUB):
            words = src[pl.ds(base + s, NPICK, stride=SUB), :]
            u = pltpu.bitcast(words << 16, F32)
            vs[s] = pltpu.bitcast(words & HIGH_HALF, F32)
            r = r + u * xn[:, s * LANE:(s + 1) * LANE]
        dots = jnp.sum(r, axis=1, keepdims=True)
        gcol = jnp.sum(jnp.where(tok_lane == t, gates_ref[...], 0.0), axis=1, keepdims=True)
        coef = gcol * jax.nn.gelu(dots)
        ffn = jnp.concatenate(
            [jnp.sum(coef * vs[s], axis=0, keepdims=True) for s in range(SUB)], axis=1)
        o_ref[pl.ds(t, 1), :] = x + ffn

    def step(src, ssem, dst, dsem, first):
        if first:
            @pl.when(i == 0)
            def _():
                def prologue(t, c):
                    issue_token(idx_cur, t, src, ssem)
                    return c
                lax.fori_loop(0, tt, prologue, 0)

        pltpu.make_async_copy(dst, src, ssem).wait()

        def tokens(with_issue):
            def body(j, c):
                for r in range(unroll):
                    t = j * unroll + r
                    if with_issue:
                        issue_token(idx_nxt, t, dst, dsem)
                    compute_token(t, src, vscr.at[r])
                return c
            lax.fori_loop(0, tt // unroll, body, 0)

        @pl.when(i + 1 < nsteps)
        def _():
            tokens(True)

        @pl.when(i + 1 == nsteps)
        def _():
            tokens(False)

    @pl.when(i % 2 == 0)
    def _():
        step(buf0, sem.at[0], buf1, sem.at[1], True)

    @pl.when(i % 2 == 1)
    def _():
        step(buf1, sem.at[1], buf0, sem.at[0], False)


def pack_uv(u, v):
    E, D = u.shape
    ub = lax.bitcast_convert_type(u.astype(BF16), jnp.uint16).astype(jnp.uint32)
    vb = lax.bitcast_convert_type(v.astype(BF16), jnp.uint16).astype(jnp.uint32)
    return lax.bitcast_convert_type(ub | (vb << 16), jnp.int32).reshape(E, SUB, D // SUB)


def peer_expert(h, g, eidx, gates_t, tab, *, tt=16, unroll=2):
    T, D = h.shape
    nsteps = T // tt
    body = functools.partial(_peer_expert_body, tt=tt, nsteps=nsteps, unroll=unroll)
    return pl.pallas_call(
        body,
        grid=(nsteps,),
        in_specs=[
            pl.BlockSpec((tt, NPICK), lambda i: (i, 0), memory_space=pltpu.SMEM),
            pl.BlockSpec((tt, NPICK), lambda i: (jnp.minimum(i + 1, nsteps - 1), 0), memory_space=pltpu.SMEM),
            pl.BlockSpec((tt, D), lambda i: (i, 0)),
            _resident((1, D)),
            pl.BlockSpec((None, NPICK, tt), lambda i: (i, 0, 0)),
            pl.BlockSpec(memory_space=pl.ANY),
        ],
        out_specs=pl.BlockSpec((tt, D), lambda i: (i, 0)),
        out_shape=jax.ShapeDtypeStruct((T, D), F32),
        scratch_shapes=[
            pltpu.VMEM((tt * NPICK * SUB, LANE), jnp.int32),
            pltpu.VMEM((tt * NPICK * SUB, LANE), jnp.int32),
            pltpu.VMEM((unroll, SUB, NPICK, LANE), F32),
            pltpu.SemaphoreType.DMA((2,)),
        ],
        compiler_params=_params("arbitrary"),
        name="peer_expert",
    )(eidx, eidx, h, g.reshape(1, D), gates_t, tab)


def peer_ffn(h, g, w_q, sub_keys, u_tab, v_tab, *, tt=16):
    T = h.shape[0]
    eidx_t, gates_t = peer_route(h, g, w_q, sub_keys)
    gates_t = gates_t.reshape(NPICK, T // tt, tt).transpose(1, 0, 2)
    return peer_expert(h, g, eidx_t.T, gates_t, pack_uv(u_tab, v_tab), tt=tt)


def _ple_body(h_ref, p_ref, g_ref, wg_ref, wp_ref, gf_ref, o_ref, *, final):
    x = h_ref[...]
    gate = jax.nn.sigmoid(jnp.dot(_rms(x, g_ref[...]).astype(BF16), wg_ref[...], preferred_element_type=F32))
    emb = jnp.dot(p_ref[...].astype(BF16), wp_ref[...], preferred_element_type=F32)
    y = x + gate * emb
    o_ref[...] = _rms(y, gf_ref[...]) if final else y


def per_layer_embed(h, p, g, w_gate, w_proj, g_final, *, final, tm=512):
    T, D = h.shape
    P = p.shape[1]
    tile = pl.BlockSpec((tm, D), lambda i: (i, 0))
    return pl.pallas_call(
        functools.partial(_ple_body, final=final),
        grid=(T // tm,),
        in_specs=[tile, pl.BlockSpec((tm, P), lambda i: (i, 0)), _resident((1, D)), _resident((D, D)),
                  _resident((P, D)), _resident((1, D))],
        out_specs=tile,
        out_shape=jax.ShapeDtypeStruct((T, D), F32),
        compiler_params=_params("parallel"),
        name="per_layer_embed",
    )(h, p, g.reshape(1, D), w_gate.astype(BF16), w_proj.astype(BF16), g_final.reshape(1, D))


def kernel(x, p, norm_mix, a_w_in, a_conv_w, a_conv_b, a_w_r, a_w_i, a_b_r, a_b_i, a_lambda,
           a_w_out, kv_norm, w_kv, b_w_q, b_w_o, norm_ffn, peer_w_q, peer_sub_keys, peer_u,
           peer_v, norm_ple, ple_w_gate, ple_w_proj, final_norm):
    B, S, D = x.shape
    T = B * S
    depth = norm_mix.shape[0]
    n_a = a_w_in.shape[0]
    h = x.reshape(T, D)
    k_sh = v_sh = None
    heads = lambda a: a.reshape(B, S, ATTN_HEADS, ATTN_HEAD_DIM).transpose(0, 2, 1, 3)
    for i in range(depth):
        if i < n_a:
            j = i
            h = rglru_mixer(h, norm_mix[i], a_w_in[j], a_conv_w[j], a_conv_b[j], a_w_r[j], a_w_i[j],
                            a_b_r[j], a_b_i[j], a_lambda[j], a_w_out[j], batch=B)
        else:
            j = i - n_a
            q, k_new, v_new = qkv_project(h, norm_mix[i], kv_norm, b_w_q[j], w_kv)
            if i == n_a:
                k_sh, v_sh = heads(k_new), heads(v_new)
            o = stick_breaking_attention(heads(q), k_sh, v_sh)
            h = matmul_residual(h, o.transpose(0, 2, 1, 3).reshape(T, -1), b_w_o[j])
        h = peer_ffn(h, norm_ffn[i], peer_w_q[i], peer_sub_keys[i], peer_u[i], peer_v[i])
        h = per_layer_embed(h, p[i].reshape(T, -1), norm_ple[i], ple_w_gate[i], ple_w_proj[i], final_norm,
                            final=(i == depth - 1))
    return h.reshape(B, S, D)
```

```python
import functools

import jax
import jax.numpy as jnp
from jax import lax
from jax.experimental import pallas as pl
from jax.experimental.pallas import tpu as pltpu

F32 = jnp.float32
BF16 = jnp.bfloat16

EPS = 1e-6
LRU_HEADS = 8
CONV_WIDTH = 4
LRU_C = 8.0
ATTN_HEADS = 16
ATTN_HEAD_DIM = 64
PEER_HEADS = 8
PEER_NKEYS = 128
PEER_TOPK = 16

SUB, LANE = 8, 128
HIGH_HALF = -65536
VMEM_LIMIT = 48 * 1024 * 1024


def _params(*semantics):
    return pltpu.CompilerParams(dimension_semantics=semantics, vmem_limit_bytes=VMEM_LIMIT)


def _rms(x, g):
    return x * lax.rsqrt(jnp.mean(x * x, axis=-1, keepdims=True) + EPS) * g


def _softplus(z):
    return jnp.maximum(z, 0.0) + jnp.log1p(jnp.exp(-jnp.abs(z)))


def _resident(shape):
    return pl.BlockSpec(shape, lambda *_: (0,) * len(shape))


def _rglru_body(h_ref, g_ref, win_ref, cw_ref, cb_ref, wr_ref, wi_ref, br_ref, bi_ref, lam_ref,
                wout_ref, o_ref, tail_ref, state_ref):
    ts, _ = h_ref.shape
    width = wout_ref.shape[0]
    blk = width // LRU_HEADS

    @pl.when(pl.program_id(1) == 0)
    def _():
        tail_ref[...] = jnp.zeros_like(tail_ref)
        state_ref[...] = jnp.zeros_like(state_ref)

    x = h_ref[...]
    proj = jnp.dot(_rms(x, g_ref[...]).astype(BF16), win_ref[...], preferred_element_type=F32)
    yb, xb = proj[:, :width], proj[:, width:]

    row = lax.broadcasted_iota(jnp.int32, (ts, width), 0)
    ext = jnp.concatenate([tail_ref[...], xb], axis=0)
    xc = cw_ref[CONV_WIDTH - 1:CONV_WIDTH, :] * xb + cb_ref[...]
    for d in range(1, CONV_WIDTH):
        xc = xc + cw_ref[CONV_WIDTH - 1 - d:CONV_WIDTH - d, :] * pltpu.roll(ext, d, axis=0)[SUB:, :]
    tail_ref[...] = xb[ts - SUB:, :]

    xcb = xc.astype(BF16)
    r = jnp.concatenate([jnp.dot(xcb[:, k * blk:(k + 1) * blk], wr_ref[k], preferred_element_type=F32)
                         for k in range(LRU_HEADS)], axis=1)
    ig = jnp.concatenate([jnp.dot(xcb[:, k * blk:(k + 1) * blk], wi_ref[k], preferred_element_type=F32)
                          for k in range(LRU_HEADS)], axis=1)
    r = jax.nn.sigmoid(r + br_ref[...])
    ig = jax.nn.sigmoid(ig + bi_ref[...])
    log_a = -LRU_C * r * _softplus(-lam_ref[...])
    a = jnp.exp(log_a)
    u = jnp.sqrt(-jnp.tanh(log_a) * (a * a + 1.0)) * ig * xc

    d = 1
    while d < ts:
        keep = row >= d
        a_prev = jnp.where(keep, pltpu.roll(a, d, axis=0), 1.0)
        u_prev = jnp.where(keep, pltpu.roll(u, d, axis=0), 0.0)
        u = a * u_prev + u
        a = a * a_prev
        d *= 2
    hseq = u + a * state_ref[SUB - 1:SUB, :]
    state_ref[...] = hseq[ts - SUB:, :]

    gated = (jax.nn.gelu(yb) * hseq).astype(BF16)
    o_ref[...] = x + jnp.dot(gated, wout_ref[...], preferred_element_type=F32)


def rglru_mixer(h, g, w_in, conv_w, conv_b, w_r, w_i, b_r, b_i, lam, w_out, *, batch, ts=256):
    T, D = h.shape
    S = T // batch
    nt = S // ts
    W = w_out.shape[0]
    tile = pl.BlockSpec((ts, D), lambda b, j: (b * nt + j, 0))
    vec = lambda a: a.reshape(1, -1)
    return pl.pallas_call(
        _rglru_body,
        grid=(batch, nt),
        in_specs=[tile, _resident((1, D)), _resident((D, 2 * W)), _resident((CONV_WIDTH, W)),
                  _resident((1, W)), _resident(w_r.shape), _resident(w_i.shape), _resident((1, W)),
                  _resident((1, W)), _resident((1, W)), _resident((W, D))],
        out_specs=tile,
        out_shape=jax.ShapeDtypeStruct((T, D), F32),
        scratch_shapes=[pltpu.VMEM((SUB, W), F32), pltpu.VMEM((SUB, W), F32)],
        compiler_params=_params("parallel", "arbitrary"),
        name="rglru_mixer",
    )(h, vec(g), w_in.astype(BF16), conv_w, vec(conv_b), w_r.astype(BF16), w_i.astype(BF16),
      vec(b_r), vec(b_i), vec(lam), w_out.astype(BF16))


def _qkv_body(h_ref, gq_ref, gkv_ref, wq_ref, wkv_ref, q_ref, k_ref, v_ref, *, scale):
    x = h_ref[...]
    xhat = x * lax.rsqrt(jnp.mean(x * x, axis=-1, keepdims=True) + EPS)
    q = jnp.dot((xhat * gq_ref[...]).astype(BF16), wq_ref[...], preferred_element_type=F32)
    kv = jnp.dot((xhat * gkv_ref[...]).astype(BF16), wkv_ref[...], preferred_element_type=F32)
    aw = k_ref.shape[-1]
    q_ref[...] = (q * scale).astype(q_ref.dtype)
    k_ref[...] = kv[:, :aw].astype(k_ref.dtype)
    v_ref[...] = kv[:, aw:].astype(v_ref.dtype)


def qkv_project(h, g_q, g_kv, w_q, w_kv, *, tm=512):
    T, D = h.shape
    aw = w_q.shape[1]
    tile = pl.BlockSpec((tm, D), lambda i: (i, 0))
    out = pl.BlockSpec((tm, aw), lambda i: (i, 0))
    return pl.pallas_call(
        functools.partial(_qkv_body, scale=ATTN_HEAD_DIM ** -0.5),
        grid=(T // tm,),
        in_specs=[tile, _resident((1, D)), _resident((1, D)), _resident((D, aw)), _resident((D, 2 * aw))],
        out_specs=[out, out, out],
        out_shape=[jax.ShapeDtypeStruct((T, aw), BF16)] * 3,
        compiler_params=_params("parallel"),
        name="qkv_project",
    )(h, g_q.reshape(1, D), g_kv.reshape(1, D), w_q.astype(BF16), w_kv.astype(BF16))


ATTN_TQ, ATTN_TK = 512, 256


def _stick_body(q_ref, k_ref, v_ref, o_ref, *, tq, tk, dh):
    S, lanes = q_ref.shape
    nh = lanes // dh
    ndiag = tq // tk
    row = lax.broadcasted_iota(jnp.int32, (tk, tk), 0)
    col = lax.broadcasted_iota(jnp.int32, (tk, tk), 1)
    neg_from = jnp.where(row >= col, -1.0, 0.0).astype(BF16)
    qpos = lax.broadcasted_iota(jnp.int32, (tq, tk), 0)
    kpos = lax.broadcasted_iota(jnp.int32, (tq, tk), 1)
    head_of_lane = lax.broadcasted_iota(jnp.int32, (tq, lanes), 1) // dh

    def pair(q, k, v, carry, shift):
        acc, right = carry
        z = lax.dot_general(q, k, (((1,), (1,)), ((), ())), preferred_element_type=F32)
        sp = jnp.maximum(z, 0.0) + jnp.log(1.0 + jnp.exp(-jnp.abs(z)))
        if shift is not None:
            causal = kpos < qpos + shift
            sp = jnp.where(causal, sp, 0.0)
        hi = pltpu.bitcast(pltpu.bitcast(sp, jnp.int32) & HIGH_HALF, F32)
        lo = sp - hi
        tail = (jnp.dot(hi.astype(BF16), neg_from, preferred_element_type=F32)
                + jnp.dot(lo.astype(BF16), neg_from, preferred_element_type=F32))
        a = jnp.exp(z + tail + right)
        if shift is not None:
            a = jnp.where(causal, a, 0.0)
        acc = acc + jnp.dot(a.astype(BF16), v, preferred_element_type=F32)
        return acc, right + tail[:, 0:1]

    def pairs(qs, kb, carries, shift):
        k = k_ref[pl.ds(pl.multiple_of(kb * tk, tk), tk), :]
        v = v_ref[pl.ds(pl.multiple_of(kb * tk, tk), tk), :]
        return tuple(pair(qs[h], k, v, carries[h], shift) for h in range(nh))

    def q_block(qi, c):
        q = q_ref[pl.ds(pl.multiple_of(qi * tq, tq), tq), :]
        qs = [jnp.where(head_of_lane == h, q, jnp.zeros_like(q)) for h in range(nh)]
        carries = tuple((jnp.zeros((tq, lanes), F32), jnp.zeros((tq, 1), F32)) for _ in range(nh))
        last = (qi + 1) * ndiag - 1
        for d in range(ndiag):
            carries = pairs(qs, last - d, carries, (d + 1 - ndiag) * tk)

        def past(n, cr):
            for d in range(ndiag):
                cr = pairs(qs, (qi - n) * ndiag - 1 - d, cr, None)
            return cr

        carries = lax.fori_loop(0, qi, past, carries)
        out = carries[0][0]
        for h in range(1, nh):
            out = jnp.where(head_of_lane == h, carries[h][0], out)
        o_ref[pl.ds(pl.multiple_of(qi * tq, tq), tq), :] = out.astype(o_ref.dtype)
        return c

    lax.fori_loop(0, S // tq, q_block, 0)


def stick_breaking_attention(q, k, v, *, batch, tq=ATTN_TQ, tk=ATTN_TK):
    T, width = q.shape
    S = T // batch
    tq = min(tq, S)
    tk = min(tk, tq)
    spec = pl.BlockSpec((S, LANE), lambda b, g: (b, g))
    return pl.pallas_call(
        functools.partial(_stick_body, tq=tq, tk=tk, dh=ATTN_HEAD_DIM),
        grid=(batch, width // LANE),
        in_specs=[spec, spec, spec],
        out_specs=spec,
        out_shape=jax.ShapeDtypeStruct((T, width), BF16),
        compiler_params=_params("parallel", "parallel"),
        name="stick_breaking",
    )(q, k, v)


def _matmul_residual_body(h_ref, x_ref, w_ref, o_ref):
    o_ref[...] = h_ref[...] + jnp.dot(x_ref[...], w_ref[...], preferred_element_type=F32)


def matmul_residual(h, x, w, *, tm=512):
    T, D = h.shape
    K = x.shape[1]
    return pl.pallas_call(
        _matmul_residual_body,
        grid=(T // tm,),
        in_specs=[pl.BlockSpec((tm, D), lambda i: (i, 0)), pl.BlockSpec((tm, K), lambda i: (i, 0)),
                  _resident((K, D))],
        out_specs=pl.BlockSpec((tm, D), lambda i: (i, 0)),
        out_shape=jax.ShapeDtypeStruct((T, D), F32),
        compiler_params=_params("parallel"),
        name="matmul_residual",
    )(h, x, w.astype(BF16))


def _topk_rows(s, k, payload=None):
    n = s.shape[0]
    rows = lax.broadcasted_iota(jnp.int32, s.shape, 0)
    vals, sel = [], []
    for _ in range(k):
        best = jnp.max(s, axis=0, keepdims=True)
        pos = jnp.min(jnp.where(s == best, rows, n), axis=0, keepdims=True)
        hit = rows == pos
        vals.append(best)
        sel.append(pos if payload is None else jnp.max(jnp.where(hit, payload, -1), axis=0, keepdims=True))
        s = jnp.where(hit, -jnp.inf, s)
    return jnp.concatenate(vals, axis=0), jnp.concatenate(sel, axis=0)


def _peer_route_body(h_ref, g_ref, wqt_ref, keys_ref, eidx_ref, gates_ref):
    xn = _rms(h_ref[...], g_ref[...]).astype(BF16)
    qt = lax.dot_general(wqt_ref[...], xn, (((1,), (1,)), ((), ())), preferred_element_type=F32)
    half = keys_ref.shape[-1]
    k = PEER_TOPK
    for hd in range(PEER_HEADS):
        sv, si = [], []
        for part in range(2):
            hp = hd * 2 + part
            q = qt[hp * half:(hp + 1) * half, :].astype(BF16)
            scores = jnp.dot(keys_ref[hp], q, preferred_element_type=F32)
            v, i = _topk_rows(scores, k)
            sv.append(v)
            si.append(i)
        widths = [k // (a + 1) for a in range(k)]
        pad = -sum(widths) % SUB
        cand = jnp.concatenate([sv[0][a:a + 1, :] + sv[1][:w, :] for a, w in enumerate(widths)]
                               + [jnp.full((pad, sv[0].shape[1]), -jnp.inf, F32)], axis=0)
        pay = jnp.concatenate([si[0][a:a + 1, :] * PEER_NKEYS + si[1][:w, :] for a, w in enumerate(widths)]
                              + [jnp.zeros((pad, sv[0].shape[1]), jnp.int32)], axis=0)
        cv, ce = _topk_rows(cand, k, payload=pay)
        ex = jnp.exp(cv - cv[0:1, :])
        gates_ref[hd * k:(hd + 1) * k, :] = ex / jnp.sum(ex, axis=0, keepdims=True)
        eidx_ref[hd * k:(hd + 1) * k, :] = ce


def peer_route(h, g, w_q, sub_keys, *, tm=256):
    T, D = h.shape
    H, _, nkeys, half = sub_keys.shape
    npick = H * PEER_TOPK
    return pl.pallas_call(
        _peer_route_body,
        grid=(T // tm,),
        in_specs=[pl.BlockSpec((tm, D), lambda i: (i, 0)), _resident((1, D)),
                  _resident((H * 2 * half, D)), _resident((H * 2, nkeys, half))],
        out_specs=[pl.BlockSpec((npick, tm), lambda i: (0, i)),
                   pl.BlockSpec((npick, tm), lambda i: (0, i))],
        out_shape=[jax.ShapeDtypeStruct((npick, T), jnp.int32),
                   jax.ShapeDtypeStruct((npick, T), F32)],
        compiler_params=_params("parallel"),
        name="peer_route",
    )(h, g.reshape(1, D), w_q.T.astype(BF16), sub_keys.reshape(H * 2, nkeys, half).astype(BF16))


NPICK = PEER_HEADS * PEER_TOPK


def _peer_expert_body(idx_cur, idx_nxt, h_ref, g_ref, gates_ref, tab_ref, o_ref, buf0, buf1, vscr, sem, *,
                      tt, nsteps, unroll):
    i = pl.program_id(0)
    tok_lane = lax.broadcasted_iota(jnp.int32, (NPICK, tt), 1)
    g = g_ref[...]

    def issue_token(idx_ref, t, dst, dsem):
        for k in range(NPICK):
            row = pl.multiple_of((t * NPICK + k) * SUB, SUB)
            pltpu.make_async_copy(tab_ref.at[idx_ref[t, k]], dst.at[pl.ds(row, SUB), :],
                                  dsem).start(priority=k % 2)

    def compute_token(t, src, vs):
        x = h_ref[pl.ds(t, 1), :]
        xn = _rms(x, g)
        base = pl.multiple_of(t * (NPICK * SUB), SUB)
        r = jnp.zeros((NPICK, LANE), F32)
        for s in range(SUB):
            words = src[pl.ds(base + s, NPICK, stride=SUB), :]
            u = pltpu.bitcast(words << 16, F32)
            vs[s] = pltpu.bitcast(words & HIGH_HALF, F32)
            r = r + u * xn[:, s * LANE:(s + 1) * LANE]
        dots = jnp.sum(r, axis=1, keepdims=True)
        gcol = jnp.sum(jnp.where(tok_lane == t, gates_ref[...], 0.0), axis=1, keepdims=True)
        coef = gcol * jax.nn.gelu(dots)
        ffn = jnp.concatenate(
            [jnp.sum(coef * vs[s], axis=0, keepdims=True) for s in range(SUB)], axis=1)
        o_ref[pl.ds(t, 1), :] = x + ffn

    def step(src, ssem, dst, dsem, first):
        if first:
            @pl.when(i == 0)
            def _():
                def prologue(t, c):
                    issue_token(idx_cur, t, src, ssem)
                    return c
                lax.fori_loop(0, tt, prologue, 0)

        pltpu.make_async_copy(dst, src, ssem).wait()

        def tokens(with_issue):
            def body(j, c):
                for r in range(unroll):
                    t = j * unroll + r
                    if with_issue:
                        issue_token(idx_nxt, t, dst, dsem)
                    compute_token(t, src, vscr.at[r])
                return c
            lax.fori_loop(0, tt // unroll, body, 0)

        @pl.when(i + 1 < nsteps)
        def _():
            tokens(True)

        @pl.when(i + 1 == nsteps)
        def _():
            tokens(False)

    @pl.when(i % 2 == 0)
    def _():
        step(buf0, sem.at[0], buf1, sem.at[1], True)

    @pl.when(i % 2 == 1)
    def _():
        step(buf1, sem.at[1], buf0, sem.at[0], False)


def pack_uv(u, v):
    E, D = u.shape
    ub = lax.bitcast_convert_type(u.astype(BF16), jnp.uint16).astype(jnp.uint32)
    vb = lax.bitcast_convert_type(v.astype(BF16), jnp.uint16).astype(jnp.uint32)
    return lax.bitcast_convert_type(ub | (vb << 16), jnp.int32).reshape(E, SUB, D // SUB)


def peer_expert(h, g, eidx, gates_t, tab, *, tt=16, unroll=2):
    T, D = h.shape
    nsteps = T // tt
    body = functools.partial(_peer_expert_body, tt=tt, nsteps=nsteps, unroll=unroll)
    return pl.pallas_call(
        body,
        grid=(nsteps,),
        in_specs=[
            pl.BlockSpec((tt, NPICK), lambda i: (i, 0), memory_space=pltpu.SMEM),
            pl.BlockSpec((tt, NPICK), lambda i: (jnp.minimum(i + 1, nsteps - 1), 0), memory_space=pltpu.SMEM),
            pl.BlockSpec((tt, D), lambda i: (i, 0)),
            _resident((1, D)),
            pl.BlockSpec((None, NPICK, tt), lambda i: (i, 0, 0)),
            pl.BlockSpec(memory_space=pl.ANY),
        ],
        out_specs=pl.BlockSpec((tt, D), lambda i: (i, 0)),
        out_shape=jax.ShapeDtypeStruct((T, D), F32),
        scratch_shapes=[
            pltpu.VMEM((tt * NPICK * SUB, LANE), jnp.int32),
            pltpu.VMEM((tt * NPICK * SUB, LANE), jnp.int32),
            pltpu.VMEM((unroll, SUB, NPICK, LANE), F32),
            pltpu.SemaphoreType.DMA((2,)),
        ],
        compiler_params=_params("arbitrary"),
        name="peer_expert",
    )(eidx, eidx, h, g.reshape(1, D), gates_t, tab)


def peer_ffn(h, g, w_q, sub_keys, u_tab, v_tab, *, tt=16):
    T = h.shape[0]
    eidx_t, gates_t = peer_route(h, g, w_q, sub_keys)
    gates_t = gates_t.reshape(NPICK, T // tt, tt).transpose(1, 0, 2)
    return peer_expert(h, g, eidx_t.T, gates_t, pack_uv(u_tab, v_tab), tt=tt)


def _ple_body(h_ref, p_ref, g_ref, wg_ref, wp_ref, gf_ref, o_ref, *, final):
    x = h_ref[...]
    gate = jax.nn.sigmoid(jnp.dot(_rms(x, g_ref[...]).astype(BF16), wg_ref[...], preferred_element_type=F32))
    emb = jnp.dot(p_ref[...].astype(BF16), wp_ref[...], preferred_element_type=F32)
    y = x + gate * emb
    o_ref[...] = _rms(y, gf_ref[...]) if final else y


def per_layer_embed(h, p, g, w_gate, w_proj, g_final, *, final, tm=512):
    T, D = h.shape
    P = p.shape[1]
    tile = pl.BlockSpec((tm, D), lambda i: (i, 0))
    return pl.pallas_call(
        functools.partial(_ple_body, final=final),
        grid=(T // tm,),
        in_specs=[tile, pl.BlockSpec((tm, P), lambda i: (i, 0)), _resident((1, D)), _resident((D, D)),
                  _resident((P, D)), _resident((1, D))],
        out_specs=tile,
        out_shape=jax.ShapeDtypeStruct((T, D), F32),
        compiler_params=_params("parallel"),
        name="per_layer_embed",
    )(h, p, g.reshape(1, D), w_gate.astype(BF16), w_proj.astype(BF16), g_final.reshape(1, D))


def kernel(x, p, norm_mix, a_w_in, a_conv_w, a_conv_b, a_w_r, a_w_i, a_b_r, a_b_i, a_lambda,
           a_w_out, kv_norm, w_kv, b_w_q, b_w_o, norm_ffn, peer_w_q, peer_sub_keys, peer_u,
           peer_v, norm_ple, ple_w_gate, ple_w_proj, final_norm):
    B, S, D = x.shape
    T = B * S
    depth = norm_mix.shape[0]
    n_a = a_w_in.shape[0]
    h = x.reshape(T, D)
    k_sh = v_sh = None
    for i in range(depth):
        if i < n_a:
            j = i
            h = rglru_mixer(h, norm_mix[i], a_w_in[j], a_conv_w[j], a_conv_b[j], a_w_r[j], a_w_i[j],
                            a_b_r[j], a_b_i[j], a_lambda[j], a_w_out[j], batch=B)
        else:
            j = i - n_a
            q, k_new, v_new = qkv_project(h, norm_mix[i], kv_norm, b_w_q[j], w_kv)
            if i == n_a:
                k_sh, v_sh = k_new, v_new
            o = stick_breaking_attention(q, k_sh, v_sh, batch=B)
            h = matmul_residual(h, o, b_w_o[j])
        h = peer_ffn(h, norm_ffn[i], peer_w_q[i], peer_sub_keys[i], peer_u[i], peer_v[i])
        h = per_layer_embed(h, p[i].reshape(T, -1), norm_ple[i], ple_w_gate[i], ple_w_proj[i], final_norm,
                            final=(i == depth - 1))
    return h.reshape(B, S, D)
```

```python
import functools

import jax
import jax.numpy as jnp
from jax import lax
from jax.experimental import pallas as pl
from jax.experimental.pallas import tpu as pltpu
from jax.experimental.pallas import tpu_sc as plsc

F32 = jnp.float32
BF16 = jnp.bfloat16

EPS = 1e-6
LRU_HEADS = 8
CONV_WIDTH = 4
LRU_C = 8.0
ATTN_HEADS = 16
ATTN_HEAD_DIM = 64
PEER_HEADS = 8
PEER_NKEYS = 128
PEER_TOPK = 16

SUB, LANE = 8, 128
HIGH_HALF = -65536
VMEM_LIMIT = 48 * 1024 * 1024


def _params(*semantics):
    return pltpu.CompilerParams(dimension_semantics=semantics, vmem_limit_bytes=VMEM_LIMIT)


def _rms(x, g):
    return x * lax.rsqrt(jnp.mean(x * x, axis=-1, keepdims=True) + EPS) * g


def _softplus(z):
    return jnp.maximum(z, 0.0) + jnp.log1p(jnp.exp(-jnp.abs(z)))


def _resident(shape):
    return pl.BlockSpec(shape, lambda *_: (0,) * len(shape))


def _rglru_body(h_ref, g_ref, win_ref, cw_ref, cb_ref, wr_ref, wi_ref, br_ref, bi_ref, lam_ref,
                wout_ref, o_ref, tail_ref, state_ref):
    ts, _ = h_ref.shape
    width = wout_ref.shape[0]
    blk = width // LRU_HEADS

    @pl.when(pl.program_id(1) == 0)
    def _():
        tail_ref[...] = jnp.zeros_like(tail_ref)
        state_ref[...] = jnp.zeros_like(state_ref)

    x = h_ref[...]
    proj = jnp.dot(_rms(x, g_ref[...]).astype(BF16), win_ref[...], preferred_element_type=F32)
    yb, xb = proj[:, :width], proj[:, width:]

    row = lax.broadcasted_iota(jnp.int32, (ts, width), 0)
    ext = jnp.concatenate([tail_ref[...], xb], axis=0)
    xc = cw_ref[CONV_WIDTH - 1:CONV_WIDTH, :] * xb + cb_ref[...]
    for d in range(1, CONV_WIDTH):
        xc = xc + cw_ref[CONV_WIDTH - 1 - d:CONV_WIDTH - d, :] * pltpu.roll(ext, d, axis=0)[SUB:, :]
    tail_ref[...] = xb[ts - SUB:, :]

    xcb = xc.astype(BF16)
    r = jnp.concatenate([jnp.dot(xcb[:, k * blk:(k + 1) * blk], wr_ref[k], preferred_element_type=F32)
                         for k in range(LRU_HEADS)], axis=1)
    ig = jnp.concatenate([jnp.dot(xcb[:, k * blk:(k + 1) * blk], wi_ref[k], preferred_element_type=F32)
                          for k in range(LRU_HEADS)], axis=1)
    r = jax.nn.sigmoid(r + br_ref[...])
    ig = jax.nn.sigmoid(ig + bi_ref[...])
    log_a = -LRU_C * r * _softplus(-lam_ref[...])
    a = jnp.exp(log_a)
    u = jnp.sqrt(-jnp.tanh(log_a) * (a * a + 1.0)) * ig * xc

    d = 1
    while d < ts:
        keep = row >= d
        a_prev = jnp.where(keep, pltpu.roll(a, d, axis=0), 1.0)
        u_prev = jnp.where(keep, pltpu.roll(u, d, axis=0), 0.0)
        u = a * u_prev + u
        a = a * a_prev
        d *= 2
    hseq = u + a * state_ref[SUB - 1:SUB, :]
    state_ref[...] = hseq[ts - SUB:, :]

    gated = (jax.nn.gelu(yb) * hseq).astype(BF16)
    o_ref[...] = x + jnp.dot(gated, wout_ref[...], preferred_element_type=F32)


def rglru_mixer(h, g, w_in, conv_w, conv_b, w_r, w_i, b_r, b_i, lam, w_out, *, batch, ts=256):
    T, D = h.shape
    S = T // batch
    nt = S // ts
    W = w_out.shape[0]
    tile = pl.BlockSpec((ts, D), lambda b, j: (b * nt + j, 0))
    vec = lambda a: a.reshape(1, -1)
    return pl.pallas_call(
        _rglru_body,
        grid=(batch, nt),
        in_specs=[tile, _resident((1, D)), _resident((D, 2 * W)), _resident((CONV_WIDTH, W)),
                  _resident((1, W)), _resident(w_r.shape), _resident(w_i.shape), _resident((1, W)),
                  _resident((1, W)), _resident((1, W)), _resident((W, D))],
        out_specs=tile,
        out_shape=jax.ShapeDtypeStruct((T, D), F32),
        scratch_shapes=[pltpu.VMEM((SUB, W), F32), pltpu.VMEM((SUB, W), F32)],
        compiler_params=_params("parallel", "arbitrary"),
        name="rglru_mixer",
    )(h, vec(g), w_in.astype(BF16), conv_w, vec(conv_b), w_r.astype(BF16), w_i.astype(BF16),
      vec(b_r), vec(b_i), vec(lam), w_out.astype(BF16))


def _qkv_body(h_ref, gq_ref, gkv_ref, wq_ref, wkv_ref, q_ref, k_ref, v_ref, *, scale):
    x = h_ref[...]
    xhat = x * lax.rsqrt(jnp.mean(x * x, axis=-1, keepdims=True) + EPS)
    q = jnp.dot((xhat * gq_ref[...]).astype(BF16), wq_ref[...], preferred_element_type=F32)
    kv = jnp.dot((xhat * gkv_ref[...]).astype(BF16), wkv_ref[...], preferred_element_type=F32)
    aw = k_ref.shape[-1]
    q_ref[...] = (q * scale).astype(q_ref.dtype)
    k_ref[...] = kv[:, :aw].astype(k_ref.dtype)
    v_ref[...] = kv[:, aw:].astype(v_ref.dtype)


def qkv_project(h, g_q, g_kv, w_q, w_kv, *, tm=512):
    T, D = h.shape
    aw = w_q.shape[1]
    tile = pl.BlockSpec((tm, D), lambda i: (i, 0))
    out = pl.BlockSpec((tm, aw), lambda i: (i, 0))
    return pl.pallas_call(
        functools.partial(_qkv_body, scale=ATTN_HEAD_DIM ** -0.5),
        grid=(T // tm,),
        in_specs=[tile, _resident((1, D)), _resident((1, D)), _resident((D, aw)), _resident((D, 2 * aw))],
        out_specs=[out, out, out],
        out_shape=[jax.ShapeDtypeStruct((T, aw), BF16)] * 3,
        compiler_params=_params("parallel"),
        name="qkv_project",
    )(h, g_q.reshape(1, D), g_kv.reshape(1, D), w_q.astype(BF16), w_kv.astype(BF16))


ATTN_TQ, ATTN_TK = 512, 256


def _stick_body(q_ref, k_ref, v_ref, o_ref, *, tq, tk, dh):
    S, lanes = q_ref.shape
    nh = lanes // dh
    ndiag = tq // tk
    row = lax.broadcasted_iota(jnp.int32, (tk, tk), 0)
    col = lax.broadcasted_iota(jnp.int32, (tk, tk), 1)
    neg_from = jnp.where(row >= col, -1.0, 0.0).astype(BF16)
    qpos = lax.broadcasted_iota(jnp.int32, (tq, tk), 0)
    kpos = lax.broadcasted_iota(jnp.int32, (tq, tk), 1)
    head_of_lane = lax.broadcasted_iota(jnp.int32, (tq, lanes), 1) // dh

    def pair(q, k, v, carry, shift):
        acc, right = carry
        z = lax.dot_general(q, k, (((1,), (1,)), ((), ())), preferred_element_type=F32)
        sp = jnp.maximum(z, 0.0) + jnp.log(1.0 + jnp.exp(-jnp.abs(z)))
        if shift is not None:
            causal = kpos < qpos + shift
            sp = jnp.where(causal, sp, 0.0)
        hi = pltpu.bitcast(pltpu.bitcast(sp, jnp.int32) & HIGH_HALF, F32)
        lo = sp - hi
        tail = (jnp.dot(hi.astype(BF16), neg_from, preferred_element_type=F32)
                + jnp.dot(lo.astype(BF16), neg_from, preferred_element_type=F32))
        a = jnp.exp(z + tail + right)
        if shift is not None:
            a = jnp.where(causal, a, 0.0)
        acc = acc + jnp.dot(a.astype(BF16), v, preferred_element_type=F32)
        return acc, right + tail[:, 0:1]

    def pairs(qs, kb, carries, shift):
        k = k_ref[pl.ds(pl.multiple_of(kb * tk, tk), tk), :]
        v = v_ref[pl.ds(pl.multiple_of(kb * tk, tk), tk), :]
        return tuple(pair(qs[h], k, v, carries[h], shift) for h in range(nh))

    def q_block(qi, c):
        q = q_ref[pl.ds(pl.multiple_of(qi * tq, tq), tq), :]
        qs = [jnp.where(head_of_lane == h, q, jnp.zeros_like(q)) for h in range(nh)]
        carries = tuple((jnp.zeros((tq, lanes), F32), jnp.zeros((tq, 1), F32)) for _ in range(nh))
        last = (qi + 1) * ndiag - 1
        for d in range(ndiag):
            carries = pairs(qs, last - d, carries, (d + 1 - ndiag) * tk)

        def past(n, cr):
            for d in range(ndiag):
                cr = pairs(qs, (qi - n) * ndiag - 1 - d, cr, None)
            return cr

        carries = lax.fori_loop(0, qi, past, carries)
        out = carries[0][0]
        for h in range(1, nh):
            out = jnp.where(head_of_lane == h, carries[h][0], out)
        o_ref[pl.ds(pl.multiple_of(qi * tq, tq), tq), :] = out.astype(o_ref.dtype)
        return c

    lax.fori_loop(0, S // tq, q_block, 0)


def stick_breaking_attention(q, k, v, *, batch, tq=ATTN_TQ, tk=ATTN_TK):
    T, width = q.shape
    S = T // batch
    tq = min(tq, S)
    tk = min(tk, tq)
    spec = pl.BlockSpec((S, LANE), lambda b, g: (b, g))
    return pl.pallas_call(
        functools.partial(_stick_body, tq=tq, tk=tk, dh=ATTN_HEAD_DIM),
        grid=(batch, width // LANE),
        in_specs=[spec, spec, spec],
        out_specs=spec,
        out_shape=jax.ShapeDtypeStruct((T, width), BF16),
        compiler_params=_params("parallel", "parallel"),
        name="stick_breaking",
    )(q, k, v)


def _matmul_residual_body(h_ref, x_ref, w_ref, o_ref):
    o_ref[...] = h_ref[...] + jnp.dot(x_ref[...], w_ref[...], preferred_element_type=F32)


def matmul_residual(h, x, w, *, tm=512):
    T, D = h.shape
    K = x.shape[1]
    return pl.pallas_call(
        _matmul_residual_body,
        grid=(T // tm,),
        in_specs=[pl.BlockSpec((tm, D), lambda i: (i, 0)), pl.BlockSpec((tm, K), lambda i: (i, 0)),
                  _resident((K, D))],
        out_specs=pl.BlockSpec((tm, D), lambda i: (i, 0)),
        out_shape=jax.ShapeDtypeStruct((T, D), F32),
        compiler_params=_params("parallel"),
        name="matmul_residual",
    )(h, x, w.astype(BF16))


def _topk_rows(s, k, payload=None):
    n = s.shape[0]
    rows = lax.broadcasted_iota(jnp.int32, s.shape, 0)
    vals, sel = [], []
    for _ in range(k):
        best = jnp.max(s, axis=0, keepdims=True)
        pos = jnp.min(jnp.where(s == best, rows, n), axis=0, keepdims=True)
        hit = rows == pos
        vals.append(best)
        sel.append(pos if payload is None else jnp.max(jnp.where(hit, payload, -1), axis=0, keepdims=True))
        s = jnp.where(hit, -jnp.inf, s)
    return jnp.concatenate(vals, axis=0), jnp.concatenate(sel, axis=0)


def _peer_route_body(h_ref, g_ref, wqt_ref, keys_ref, eidx_ref, gates_ref):
    xn = _rms(h_ref[...], g_ref[...]).astype(BF16)
    qt = lax.dot_general(wqt_ref[...], xn, (((1,), (1,)), ((), ())), preferred_element_type=F32)
    half = keys_ref.shape[-1]
    k = PEER_TOPK
    for hd in range(PEER_HEADS):
        sv, si = [], []
        for part in range(2):
            hp = hd * 2 + part
            q = qt[hp * half:(hp + 1) * half, :].astype(BF16)
            scores = jnp.dot(keys_ref[hp], q, preferred_element_type=F32)
            v, i = _topk_rows(scores, k)
            sv.append(v)
            si.append(i)
        widths = [k // (a + 1) for a in range(k)]
        pad = -sum(widths) % SUB
        cand = jnp.concatenate([sv[0][a:a + 1, :] + sv[1][:w, :] for a, w in enumerate(widths)]
                               + [jnp.full((pad, sv[0].shape[1]), -jnp.inf, F32)], axis=0)
        pay = jnp.concatenate([si[0][a:a + 1, :] * PEER_NKEYS + si[1][:w, :] for a, w in enumerate(widths)]
                              + [jnp.zeros((pad, sv[0].shape[1]), jnp.int32)], axis=0)
        cv, ce = _topk_rows(cand, k, payload=pay)
        ex = jnp.exp(cv - cv[0:1, :])
        gates_ref[hd * k:(hd + 1) * k, :] = ex / jnp.sum(ex, axis=0, keepdims=True)
        eidx_ref[hd * k:(hd + 1) * k, :] = ce


def peer_route(h, g, w_q, sub_keys, *, tm=256):
    T, D = h.shape
    H, _, nkeys, half = sub_keys.shape
    npick = H * PEER_TOPK
    return pl.pallas_call(
        _peer_route_body,
        grid=(T // tm,),
        in_specs=[pl.BlockSpec((tm, D), lambda i: (i, 0)), _resident((1, D)),
                  _resident((H * 2 * half, D)), _resident((H * 2, nkeys, half))],
        out_specs=[pl.BlockSpec((npick, tm), lambda i: (0, i)),
                   pl.BlockSpec((npick, tm), lambda i: (0, i))],
        out_shape=[jax.ShapeDtypeStruct((npick, T), jnp.int32),
                   jax.ShapeDtypeStruct((npick, T), F32)],
        compiler_params=_params("parallel"),
        name="peer_route",
    )(h, g.reshape(1, D), w_q.T.astype(BF16), sub_keys.reshape(H * 2, nkeys, half).astype(BF16))


NPICK = PEER_HEADS * PEER_TOPK


def _expert_token(t, words_of_chunk, vs, h_ref, g_ref, gates_ref, o_ref, tok_lane):
    x = h_ref[pl.ds(t, 1), :]
    xn = _rms(x, g_ref[...])
    r = jnp.zeros((NPICK, LANE), F32)
    for c in range(SUB):
        words = words_of_chunk(c)
        u = pltpu.bitcast(words << 16, F32)
        vs[c] = pltpu.bitcast(words & HIGH_HALF, F32)
        r = r + u * xn[:, c * LANE:(c + 1) * LANE]
    dots = jnp.sum(r, axis=1, keepdims=True)
    gcol = jnp.sum(jnp.where(tok_lane == t, gates_ref[...], 0.0), axis=1, keepdims=True)
    coef = gcol * jax.nn.gelu(dots)
    ffn = jnp.concatenate(
        [jnp.sum(coef * vs[c], axis=0, keepdims=True) for c in range(SUB)], axis=1)
    o_ref[pl.ds(t, 1), :] = x + ffn


def _peer_expert_body(idx_cur, idx_nxt, h_ref, g_ref, gates_ref, tab_ref, o_ref, buf0, buf1, vscr, sem, *,
                      tt, nsteps, unroll):
    i = pl.program_id(0)
    tok_lane = lax.broadcasted_iota(jnp.int32, (NPICK, tt), 1)

    def issue_token(idx_ref, t, dst, dsem):
        for k in range(NPICK):
            row = pl.multiple_of((t * NPICK + k) * SUB, SUB)
            pltpu.make_async_copy(tab_ref.at[idx_ref[t, k]], dst.at[pl.ds(row, SUB), :],
                                  dsem).start(priority=k % 2)

    def compute_token(t, src, vs):
        base = pl.multiple_of(t * (NPICK * SUB), SUB)
        _expert_token(t, lambda c: src[pl.ds(base + c, NPICK, stride=SUB), :], vs,
                      h_ref, g_ref, gates_ref, o_ref, tok_lane)

    def step(src, ssem, dst, dsem, first):
        if first:
            @pl.when(i == 0)
            def _():
                def prologue(t, c):
                    issue_token(idx_cur, t, src, ssem)
                    return c
                lax.fori_loop(0, tt, prologue, 0)

        pltpu.make_async_copy(dst, src, ssem).wait()

        def tokens(with_issue):
            def body(j, c):
                for r in range(unroll):
                    t = j * unroll + r
                    if with_issue:
                        issue_token(idx_nxt, t, dst, dsem)
                    compute_token(t, src, vscr.at[r])
                return c
            lax.fori_loop(0, tt // unroll, body, 0)

        @pl.when(i + 1 < nsteps)
        def _():
            tokens(True)

        @pl.when(i + 1 == nsteps)
        def _():
            tokens(False)

    @pl.when(i % 2 == 0)
    def _():
        step(buf0, sem.at[0], buf1, sem.at[1], True)

    @pl.when(i % 2 == 1)
    def _():
        step(buf1, sem.at[1], buf0, sem.at[0], False)


def pack_uv(u, v):
    E, D = u.shape
    ub = lax.bitcast_convert_type(u.astype(BF16), jnp.uint16).astype(jnp.uint32)
    vb = lax.bitcast_convert_type(v.astype(BF16), jnp.uint16).astype(jnp.uint32)
    return lax.bitcast_convert_type(ub | (vb << 16), jnp.int32).reshape(E, SUB, D // SUB)


def peer_expert(h, g, eidx, gates_t, tab, *, tt=16, unroll=2):
    T, D = h.shape
    nsteps = T // tt
    body = functools.partial(_peer_expert_body, tt=tt, nsteps=nsteps, unroll=unroll)
    return pl.pallas_call(
        body,
        grid=(nsteps,),
        in_specs=[
            pl.BlockSpec((tt, NPICK), lambda i: (i, 0), memory_space=pltpu.SMEM),
            pl.BlockSpec((tt, NPICK), lambda i: (jnp.minimum(i + 1, nsteps - 1), 0), memory_space=pltpu.SMEM),
            pl.BlockSpec((tt, D), lambda i: (i, 0)),
            _resident((1, D)),
            pl.BlockSpec((None, NPICK, tt), lambda i: (i, 0, 0)),
            pl.BlockSpec(memory_space=pl.ANY),
        ],
        out_specs=pl.BlockSpec((tt, D), lambda i: (i, 0)),
        out_shape=jax.ShapeDtypeStruct((T, D), F32),
        scratch_shapes=[
            pltpu.VMEM((tt * NPICK * SUB, LANE), jnp.int32),
            pltpu.VMEM((tt * NPICK * SUB, LANE), jnp.int32),
            pltpu.VMEM((unroll, SUB, NPICK, LANE), F32),
            pltpu.SemaphoreType.DMA((2,)),
        ],
        compiler_params=_params("arbitrary"),
        name="peer_expert",
    )(eidx, eidx, h, g.reshape(1, D), gates_t, tab)


SC_ROWS = 32
SC_IDX = 128


def sc_gather_rows(tab, idx):
    _, W = tab.shape
    N = idx.shape[0]
    info = plsc.get_sparse_core_info()
    nw = info.num_cores * info.num_subcores
    per_w = N // nw
    assert per_w * nw == N and per_w % SC_IDX == 0
    mesh = plsc.VectorSubcoreMesh(core_axis_name="core", subcore_axis_name="subcore")

    @functools.partial(
        pl.kernel, out_type=jax.ShapeDtypeStruct((N, W), tab.dtype), mesh=mesh,
        scratch_types=[pltpu.VMEM((SC_IDX,), jnp.int32), pltpu.VMEM((SC_ROWS, W), tab.dtype),
                       pltpu.VMEM((SC_ROWS, W), tab.dtype), pltpu.SemaphoreType.DMA, pltpu.SemaphoreType.DMA,
                       pltpu.SemaphoreType.DMA])
    def gather(tab_hbm, idx_hbm, out_hbm, idx_v, rows0, rows1, gsem, wsem0, wsem1):
        wid = lax.axis_index("subcore") * info.num_cores + lax.axis_index("core")
        base = wid * per_w
        bufs = ((rows0, wsem0), (rows1, wsem1))

        @pl.loop(0, per_w // SC_IDX)
        def _(c):
            off = base + c * SC_IDX
            pltpu.sync_copy(idx_hbm.at[pl.ds(off, SC_IDX)], idx_v)
            for j in range(SC_IDX // SC_ROWS):
                buf, wsem = bufs[j % 2]
                dst = out_hbm.at[pl.ds(off + j * SC_ROWS, SC_ROWS)]

                @pl.when((c > 0) | (j >= 2))
                def _():
                    pltpu.make_async_copy(buf, dst, wsem).wait()

                pltpu.async_copy(tab_hbm.at[idx_v.at[pl.ds(j * SC_ROWS, SC_ROWS)]], buf, gsem).wait()
                pltpu.async_copy(buf, dst, wsem)

        for buf, wsem in bufs:
            pltpu.make_async_copy(buf, out_hbm.at[pl.ds(base, SC_ROWS)], wsem).wait()

    return gather(tab, idx)


def _peer_staged_body(h_ref, g_ref, gates_ref, rows_ref, o_ref, vscr, *, tt, unroll):
    tok_lane = lax.broadcasted_iota(jnp.int32, (NPICK, tt), 1)

    def body(j, c):
        for r in range(unroll):
            t = j * unroll + r
            base = pl.multiple_of(t * NPICK, NPICK)
            _expert_token(t, lambda c_: rows_ref[pl.ds(base, NPICK), c_ * LANE:(c_ + 1) * LANE], vscr.at[r],
                          h_ref, g_ref, gates_ref, o_ref, tok_lane)
        return c

    lax.fori_loop(0, tt // unroll, body, 0)


def peer_expert_staged(h, g, gates_t, rows, *, tt=8, unroll=2):
    T, D = h.shape
    nsteps = T // tt
    return pl.pallas_call(
        functools.partial(_peer_staged_body, tt=tt, unroll=unroll),
        grid=(nsteps,),
        in_specs=[pl.BlockSpec((tt, D), lambda i: (i, 0)), _resident((1, D)),
                  pl.BlockSpec((None, NPICK, tt), lambda i: (i, 0, 0)),
                  pl.BlockSpec((tt * NPICK, D), lambda i: (i, 0))],
        out_specs=pl.BlockSpec((tt, D), lambda i: (i, 0)),
        out_shape=jax.ShapeDtypeStruct((T, D), F32),
        scratch_shapes=[pltpu.VMEM((unroll, SUB, NPICK, LANE), F32)],
        compiler_params=_params("parallel"),
        name="peer_expert_staged",
    )(h, g.reshape(1, D), gates_t, rows)


SC_SHARE = 0.375


def _tile_gates(gates_t, tt):
    npick, T = gates_t.shape
    return gates_t.reshape(npick, T // tt, tt).transpose(1, 0, 2)


def peer_ffn(h, g, w_q, sub_keys, u_tab, v_tab, *, tt=16, tt_staged=8):
    T, D = h.shape
    eidx_t, gates_t = peer_route(h, g, w_q, sub_keys)
    eidx = eidx_t.T
    tab = pack_uv(u_tab, v_tab)
    t_sc = int(T * SC_SHARE) // 32 * 32
    t_tc = T - t_sc
    staged = sc_gather_rows(tab.reshape(tab.shape[0], D), eidx[t_tc:].reshape(-1))
    out_tc = peer_expert(h[:t_tc], g, eidx[:t_tc], _tile_gates(gates_t[:, :t_tc], tt), tab, tt=tt)
    out_sc = peer_expert_staged(h[t_tc:], g, _tile_gates(gates_t[:, t_tc:], tt_staged), staged, tt=tt_staged)
    return jnp.concatenate([out_tc, out_sc], axis=0)


def _ple_body(h_ref, p_ref, g_ref, wg_ref, wp_ref, gf_ref, o_ref, *, final):
    x = h_ref[...]
    gate = jax.nn.sigmoid(jnp.dot(_rms(x, g_ref[...]).astype(BF16), wg_ref[...], preferred_element_type=F32))
    emb = jnp.dot(p_ref[...].astype(BF16), wp_ref[...], preferred_element_type=F32)
    y = x + gate * emb
    o_ref[...] = _rms(y, gf_ref[...]) if final else y


def per_layer_embed(h, p, g, w_gate, w_proj, g_final, *, final, tm=512):
    T, D = h.shape
    P = p.shape[1]
    tile = pl.BlockSpec((tm, D), lambda i: (i, 0))
    return pl.pallas_call(
        functools.partial(_ple_body, final=final),
        grid=(T // tm,),
        in_specs=[tile, pl.BlockSpec((tm, P), lambda i: (i, 0)), _resident((1, D)), _resident((D, D)),
                  _resident((P, D)), _resident((1, D))],
        out_specs=tile,
        out_shape=jax.ShapeDtypeStruct((T, D), F32),
        compiler_params=_params("parallel"),
        name="per_layer_embed",
    )(h, p, g.reshape(1, D), w_gate.astype(BF16), w_proj.astype(BF16), g_final.reshape(1, D))


def kernel(x, p, norm_mix, a_w_in, a_conv_w, a_conv_b, a_w_r, a_w_i, a_b_r, a_b_i, a_lambda,
           a_w_out, kv_norm, w_kv, b_w_q, b_w_o, norm_ffn, peer_w_q, peer_sub_keys, peer_u,
           peer_v, norm_ple, ple_w_gate, ple_w_proj, final_norm):
    B, S, D = x.shape
    T = B * S
    depth = norm_mix.shape[0]
    n_a = a_w_in.shape[0]
    h = x.reshape(T, D)
    k_sh = v_sh = None
    for i in range(depth):
        if i < n_a:
            j = i
            h = rglru_mixer(h, norm_mix[i], a_w_in[j], a_conv_w[j], a_conv_b[j], a_w_r[j], a_w_i[j],
                            a_b_r[j], a_b_i[j], a_lambda[j], a_w_out[j], batch=B)
        else:
            j = i - n_a
            q, k_new, v_new = qkv_project(h, norm_mix[i], kv_norm, b_w_q[j], w_kv)
            if i == n_a:
                k_sh, v_sh = k_new, v_new
            o = stick_breaking_attention(q, k_sh, v_sh, batch=B)
            h = matmul_residual(h, o, b_w_o[j])
        h = peer_ffn(h, norm_ffn[i], peer_w_q[i], peer_sub_keys[i], peer_u[i], peer_v[i])
        h = per_layer_embed(h, p[i].reshape(T, -1), norm_ple[i], ple_w_gate[i], ple_w_proj[i], final_norm,
                            final=(i == depth - 1))
    return h.reshape(B, S, D)
```

```python
import functools

import jax
import jax.numpy as jnp
from jax import lax
from jax.experimental import pallas as pl
from jax.experimental.pallas import tpu as pltpu
from jax.experimental.pallas import tpu_sc as plsc

F32 = jnp.float32
BF16 = jnp.bfloat16

EPS = 1e-6
LRU_HEADS = 8
CONV_WIDTH = 4
LRU_C = 8.0
ATTN_HEADS = 16
ATTN_HEAD_DIM = 64
PEER_HEADS = 8
PEER_NKEYS = 128
PEER_TOPK = 16

SUB, LANE = 8, 128
HIGH_HALF = -65536
VMEM_LIMIT = 48 * 1024 * 1024


def _params(*semantics):
    return pltpu.CompilerParams(dimension_semantics=semantics, vmem_limit_bytes=VMEM_LIMIT)


def _rms(x, g):
    return x * lax.rsqrt(jnp.mean(x * x, axis=-1, keepdims=True) + EPS) * g


def _softplus(z):
    return jnp.maximum(z, 0.0) + jnp.log1p(jnp.exp(-jnp.abs(z)))


def _resident(shape):
    return pl.BlockSpec(shape, lambda *_: (0,) * len(shape))


def _rglru_body(h_ref, g_ref, win_ref, cw_ref, cb_ref, wr_ref, wi_ref, br_ref, bi_ref, lam_ref,
                wout_ref, o_ref, tail_ref, state_ref):
    ts, _ = h_ref.shape
    width = wout_ref.shape[0]
    blk = width // LRU_HEADS

    @pl.when(pl.program_id(1) == 0)
    def _():
        tail_ref[...] = jnp.zeros_like(tail_ref)
        state_ref[...] = jnp.zeros_like(state_ref)

    x = h_ref[...]
    proj = jnp.dot(_rms(x, g_ref[...]).astype(BF16), win_ref[...], preferred_element_type=F32)
    yb, xb = proj[:, :width], proj[:, width:]

    row = lax.broadcasted_iota(jnp.int32, (ts, width), 0)
    ext = jnp.concatenate([tail_ref[...], xb], axis=0)
    xc = cw_ref[CONV_WIDTH - 1:CONV_WIDTH, :] * xb + cb_ref[...]
    for d in range(1, CONV_WIDTH):
        xc = xc + cw_ref[CONV_WIDTH - 1 - d:CONV_WIDTH - d, :] * pltpu.roll(ext, d, axis=0)[SUB:, :]
    tail_ref[...] = xb[ts - SUB:, :]

    xcb = xc.astype(BF16)
    r = jnp.concatenate([jnp.dot(xcb[:, k * blk:(k + 1) * blk], wr_ref[k], preferred_element_type=F32)
                         for k in range(LRU_HEADS)], axis=1)
    ig = jnp.concatenate([jnp.dot(xcb[:, k * blk:(k + 1) * blk], wi_ref[k], preferred_element_type=F32)
                          for k in range(LRU_HEADS)], axis=1)
    r = jax.nn.sigmoid(r + br_ref[...])
    ig = jax.nn.sigmoid(ig + bi_ref[...])
    log_a = -LRU_C * r * _softplus(-lam_ref[...])
    a = jnp.exp(log_a)
    u = jnp.sqrt(-jnp.tanh(log_a) * (a * a + 1.0)) * ig * xc

    d = 1
    while d < ts:
        keep = row >= d
        a_prev = jnp.where(keep, pltpu.roll(a, d, axis=0), 1.0)
        u_prev = jnp.where(keep, pltpu.roll(u, d, axis=0), 0.0)
        u = a * u_prev + u
        a = a * a_prev
        d *= 2
    hseq = u + a * state_ref[SUB - 1:SUB, :]
    state_ref[...] = hseq[ts - SUB:, :]

    gated = (jax.nn.gelu(yb) * hseq).astype(BF16)
    o_ref[...] = x + jnp.dot(gated, wout_ref[...], preferred_element_type=F32)


def rglru_mixer(h, g, w_in, conv_w, conv_b, w_r, w_i, b_r, b_i, lam, w_out, *, batch, ts=256):
    T, D = h.shape
    S = T // batch
    nt = S // ts
    W = w_out.shape[0]
    tile = pl.BlockSpec((ts, D), lambda b, j: (b * nt + j, 0))
    vec = lambda a: a.reshape(1, -1)
    return pl.pallas_call(
        _rglru_body,
        grid=(batch, nt),
        in_specs=[tile, _resident((1, D)), _resident((D, 2 * W)), _resident((CONV_WIDTH, W)),
                  _resident((1, W)), _resident(w_r.shape), _resident(w_i.shape), _resident((1, W)),
                  _resident((1, W)), _resident((1, W)), _resident((W, D))],
        out_specs=tile,
        out_shape=jax.ShapeDtypeStruct((T, D), F32),
        scratch_shapes=[pltpu.VMEM((SUB, W), F32), pltpu.VMEM((SUB, W), F32)],
        compiler_params=_params("parallel", "arbitrary"),
        name="rglru_mixer",
    )(h, vec(g), w_in.astype(BF16), conv_w, vec(conv_b), w_r.astype(BF16), w_i.astype(BF16),
      vec(b_r), vec(b_i), vec(lam), w_out.astype(BF16))


def _qkv_body(h_ref, gq_ref, gkv_ref, wq_ref, wkv_ref, q_ref, k_ref, v_ref, *, scale):
    x = h_ref[...]
    xhat = x * lax.rsqrt(jnp.mean(x * x, axis=-1, keepdims=True) + EPS)
    q = jnp.dot((xhat * gq_ref[...]).astype(BF16), wq_ref[...], preferred_element_type=F32)
    kv = jnp.dot((xhat * gkv_ref[...]).astype(BF16), wkv_ref[...], preferred_element_type=F32)
    aw = k_ref.shape[-1]
    q_ref[...] = (q * scale).astype(q_ref.dtype)
    k_ref[...] = kv[:, :aw].astype(k_ref.dtype)
    v_ref[...] = kv[:, aw:].astype(v_ref.dtype)


def qkv_project(h, g_q, g_kv, w_q, w_kv, *, tm=512):
    T, D = h.shape
    aw = w_q.shape[1]
    tile = pl.BlockSpec((tm, D), lambda i: (i, 0))
    out = pl.BlockSpec((tm, aw), lambda i: (i, 0))
    return pl.pallas_call(
        functools.partial(_qkv_body, scale=ATTN_HEAD_DIM ** -0.5),
        grid=(T // tm,),
        in_specs=[tile, _resident((1, D)), _resident((1, D)), _resident((D, aw)), _resident((D, 2 * aw))],
        out_specs=[out, out, out],
        out_shape=[jax.ShapeDtypeStruct((T, aw), BF16)] * 3,
        compiler_params=_params("parallel"),
        name="qkv_project",
    )(h, g_q.reshape(1, D), g_kv.reshape(1, D), w_q.astype(BF16), w_kv.astype(BF16))


ATTN_TQ, ATTN_TK = 512, 256


def _stick_body(q_ref, k_ref, v_ref, o_ref, *, tq, tk, dh):
    S, lanes = q_ref.shape
    nh = lanes // dh
    ndiag = tq // tk
    row = lax.broadcasted_iota(jnp.int32, (tk, tk), 0)
    col = lax.broadcasted_iota(jnp.int32, (tk, tk), 1)
    neg_from = jnp.where(row >= col, -1.0, 0.0).astype(BF16)
    qpos = lax.broadcasted_iota(jnp.int32, (tq, tk), 0)
    kpos = lax.broadcasted_iota(jnp.int32, (tq, tk), 1)
    head_of_lane = lax.broadcasted_iota(jnp.int32, (tq, lanes), 1) // dh

    def pair(q, k, v, carry, shift):
        acc, right = carry
        z = lax.dot_general(q, k, (((1,), (1,)), ((), ())), preferred_element_type=F32)
        sp = jnp.maximum(z, 0.0) + jnp.log(1.0 + jnp.exp(-jnp.abs(z)))
        if shift is not None:
            causal = kpos < qpos + shift
            sp = jnp.where(causal, sp, 0.0)
        hi = pltpu.bitcast(pltpu.bitcast(sp, jnp.int32) & HIGH_HALF, F32)
        lo = sp - hi
        tail = (jnp.dot(hi.astype(BF16), neg_from, preferred_element_type=F32)
                + jnp.dot(lo.astype(BF16), neg_from, preferred_element_type=F32))
        a = jnp.exp(z + tail + right)
        if shift is not None:
            a = jnp.where(causal, a, 0.0)
        acc = acc + jnp.dot(a.astype(BF16), v, preferred_element_type=F32)
        return acc, right + tail[:, 0:1]

    def pairs(qs, kb, carries, shift):
        k = k_ref[pl.ds(pl.multiple_of(kb * tk, tk), tk), :]
        v = v_ref[pl.ds(pl.multiple_of(kb * tk, tk), tk), :]
        return tuple(pair(qs[h], k, v, carries[h], shift) for h in range(nh))

    def q_block(qi, c):
        q = q_ref[pl.ds(pl.multiple_of(qi * tq, tq), tq), :]
        qs = [jnp.where(head_of_lane == h, q, jnp.zeros_like(q)) for h in range(nh)]
        carries = tuple((jnp.zeros((tq, lanes), F32), jnp.zeros((tq, 1), F32)) for _ in range(nh))
        last = (qi + 1) * ndiag - 1
        for d in range(ndiag):
            carries = pairs(qs, last - d, carries, (d + 1 - ndiag) * tk)

        def past(n, cr):
            for d in range(ndiag):
                cr = pairs(qs, (qi - n) * ndiag - 1 - d, cr, None)
            return cr

        carries = lax.fori_loop(0, qi, past, carries)
        out = carries[0][0]
        for h in range(1, nh):
            out = jnp.where(head_of_lane == h, carries[h][0], out)
        o_ref[pl.ds(pl.multiple_of(qi * tq, tq), tq), :] = out.astype(o_ref.dtype)
        return c

    lax.fori_loop(0, S // tq, q_block, 0)


def stick_breaking_attention(q, k, v, *, batch, tq=ATTN_TQ, tk=ATTN_TK):
    T, width = q.shape
    S = T // batch
    tq = min(tq, S)
    tk = min(tk, tq)
    spec = pl.BlockSpec((S, LANE), lambda b, g: (b, g))
    return pl.pallas_call(
        functools.partial(_stick_body, tq=tq, tk=tk, dh=ATTN_HEAD_DIM),
        grid=(batch, width // LANE),
        in_specs=[spec, spec, spec],
        out_specs=spec,
        out_shape=jax.ShapeDtypeStruct((T, width), BF16),
        compiler_params=_params("parallel", "parallel"),
        name="stick_breaking",
    )(q, k, v)


def _matmul_residual_body(h_ref, x_ref, w_ref, o_ref):
    o_ref[...] = h_ref[...] + jnp.dot(x_ref[...], w_ref[...], preferred_element_type=F32)


def matmul_residual(h, x, w, *, tm=512):
    T, D = h.shape
    K = x.shape[1]
    return pl.pallas_call(
        _matmul_residual_body,
        grid=(T // tm,),
        in_specs=[pl.BlockSpec((tm, D), lambda i: (i, 0)), pl.BlockSpec((tm, K), lambda i: (i, 0)),
                  _resident((K, D))],
        out_specs=pl.BlockSpec((tm, D), lambda i: (i, 0)),
        out_shape=jax.ShapeDtypeStruct((T, D), F32),
        compiler_params=_params("parallel"),
        name="matmul_residual",
    )(h, x, w.astype(BF16))


def _topk_rows(s, k, payload=None):
    n = s.shape[0]
    rows = lax.broadcasted_iota(jnp.int32, s.shape, 0)
    vals, sel = [], []
    for _ in range(k):
        best = jnp.max(s, axis=0, keepdims=True)
        pos = jnp.min(jnp.where(s == best, rows, n), axis=0, keepdims=True)
        hit = rows == pos
        vals.append(best)
        sel.append(pos if payload is None else jnp.max(jnp.where(hit, payload, -1), axis=0, keepdims=True))
        s = jnp.where(hit, -jnp.inf, s)
    return jnp.concatenate(vals, axis=0), jnp.concatenate(sel, axis=0)


def _peer_route_body(h_ref, g_ref, wqt_ref, keys_ref, eidx_ref, gates_ref):
    xn = _rms(h_ref[...], g_ref[...]).astype(BF16)
    qt = lax.dot_general(wqt_ref[...], xn, (((1,), (1,)), ((), ())), preferred_element_type=F32)
    half = keys_ref.shape[-1]
    k = PEER_TOPK
    for hd in range(PEER_HEADS):
        sv, si = [], []
        for part in range(2):
            hp = hd * 2 + part
            q = qt[hp * half:(hp + 1) * half, :].astype(BF16)
            scores = jnp.dot(keys_ref[hp], q, preferred_element_type=F32)
            v, i = _topk_rows(scores, k)
            sv.append(v)
            si.append(i)
        widths = [k // (a + 1) for a in range(k)]
        pad = -sum(widths) % SUB
        cand = jnp.concatenate([sv[0][a:a + 1, :] + sv[1][:w, :] for a, w in enumerate(widths)]
                               + [jnp.full((pad, sv[0].shape[1]), -jnp.inf, F32)], axis=0)
        pay = jnp.concatenate([si[0][a:a + 1, :] * PEER_NKEYS + si[1][:w, :] for a, w in enumerate(widths)]
                              + [jnp.zeros((pad, sv[0].shape[1]), jnp.int32)], axis=0)
        cv, ce = _topk_rows(cand, k, payload=pay)
        ex = jnp.exp(cv - cv[0:1, :])
        gates_ref[hd * k:(hd + 1) * k, :] = ex / jnp.sum(ex, axis=0, keepdims=True)
        eidx_ref[hd * k:(hd + 1) * k, :] = ce


def peer_route(h, g, w_q, sub_keys, *, tm=256):
    T, D = h.shape
    H, _, nkeys, half = sub_keys.shape
    npick = H * PEER_TOPK
    return pl.pallas_call(
        _peer_route_body,
        grid=(T // tm,),
        in_specs=[pl.BlockSpec((tm, D), lambda i: (i, 0)), _resident((1, D)),
                  _resident((H * 2 * half, D)), _resident((H * 2, nkeys, half))],
        out_specs=[pl.BlockSpec((npick, tm), lambda i: (0, i)),
                   pl.BlockSpec((npick, tm), lambda i: (0, i))],
        out_shape=[jax.ShapeDtypeStruct((npick, T), jnp.int32),
                   jax.ShapeDtypeStruct((npick, T), F32)],
        compiler_params=_params("parallel"),
        name="peer_route",
    )(h, g.reshape(1, D), w_q.T.astype(BF16), sub_keys.reshape(H * 2, nkeys, half).astype(BF16))


NPICK = PEER_HEADS * PEER_TOPK


def _expert_token(t, words_of_chunk, vs, h_ref, g_ref, gates_ref, o_ref, tok_lane):
    x = h_ref[pl.ds(t, 1), :]
    xn = _rms(x, g_ref[...])
    r = jnp.zeros((NPICK, LANE), F32)
    for c in range(SUB):
        words = words_of_chunk(c)
        u = pltpu.bitcast(words << 16, F32)
        vs[c] = pltpu.bitcast(words & HIGH_HALF, F32)
        r = r + u * xn[:, c * LANE:(c + 1) * LANE]
    dots = jnp.sum(r, axis=1, keepdims=True)
    gcol = jnp.sum(jnp.where(tok_lane == t, gates_ref[...], 0.0), axis=1, keepdims=True)
    coef = gcol * jax.nn.gelu(dots)
    ffn = jnp.concatenate(
        [jnp.sum(coef * vs[c], axis=0, keepdims=True) for c in range(SUB)], axis=1)
    o_ref[pl.ds(t, 1), :] = x + ffn


def _peer_expert_body(*refs, tt, nsteps, unroll, staged):
    if staged:
        (idx_cur, idx_nxt, h_ref, g_ref, gates_ref, tab_ref, h2_ref, gates2_ref, rows_ref,
         o_ref, o2_ref, buf0, buf1, vscr, sem) = refs
    else:
        idx_cur, idx_nxt, h_ref, g_ref, gates_ref, tab_ref, o_ref, buf0, buf1, vscr, sem = refs
    i = pl.program_id(0)
    tok_lane = lax.broadcasted_iota(jnp.int32, (NPICK, tt), 1)

    def issue_token(idx_ref, t, dst, dsem):
        for k in range(NPICK):
            row = pl.multiple_of((t * NPICK + k) * SUB, SUB)
            pltpu.make_async_copy(tab_ref.at[idx_ref[t, k]], dst.at[pl.ds(row, SUB), :],
                                  dsem).start(priority=k % 2)

    def compute_token(t, src, vs):
        base = pl.multiple_of(t * (NPICK * SUB), SUB)
        _expert_token(t, lambda c: src[pl.ds(base + c, NPICK, stride=SUB), :], vs,
                      h_ref, g_ref, gates_ref, o_ref, tok_lane)

    def compute_staged(t, vs):
        base = pl.multiple_of(t * NPICK, NPICK)
        _expert_token(t, lambda c: rows_ref[pl.ds(base, NPICK), c * LANE:(c + 1) * LANE], vs,
                      h2_ref, g_ref, gates2_ref, o2_ref, tok_lane)

    def step(src, ssem, dst, dsem, first):
        if first:
            @pl.when(i == 0)
            def _():
                def prologue(t, c):
                    issue_token(idx_cur, t, src, ssem)
                    return c
                lax.fori_loop(0, tt, prologue, 0)

        pltpu.make_async_copy(dst, src, ssem).wait()

        def tokens(with_issue):
            def body(j, c):
                for r in range(unroll):
                    t = j * unroll + r
                    if with_issue:
                        issue_token(idx_nxt, t, dst, dsem)
                    compute_token(t, src, vscr.at[r])
                    if staged:
                        compute_staged(t, vscr.at[unroll + r])
                return c
            lax.fori_loop(0, tt // unroll, body, 0)

        @pl.when(i + 1 < nsteps)
        def _():
            tokens(True)

        @pl.when(i + 1 == nsteps)
        def _():
            tokens(False)

    @pl.when(i % 2 == 0)
    def _():
        step(buf0, sem.at[0], buf1, sem.at[1], True)

    @pl.when(i % 2 == 1)
    def _():
        step(buf1, sem.at[1], buf0, sem.at[0], False)


def pack_uv(u, v):
    E, D = u.shape
    ub = lax.bitcast_convert_type(u.astype(BF16), jnp.uint16).astype(jnp.uint32)
    vb = lax.bitcast_convert_type(v.astype(BF16), jnp.uint16).astype(jnp.uint32)
    return lax.bitcast_convert_type(ub | (vb << 16), jnp.int32).reshape(E, SUB, D // SUB)


def peer_expert(h, g, eidx, gates_t, tab, staged=None, *, tt=16, unroll=2):
    T, D = h.shape
    nsteps = T // tt
    tile = pl.BlockSpec((tt, D), lambda i: (i, 0))
    gate_tile = pl.BlockSpec((None, NPICK, tt), lambda i: (i, 0, 0))
    in_specs = [
        pl.BlockSpec((tt, NPICK), lambda i: (i, 0), memory_space=pltpu.SMEM),
        pl.BlockSpec((tt, NPICK), lambda i: (jnp.minimum(i + 1, nsteps - 1), 0), memory_space=pltpu.SMEM),
        tile, _resident((1, D)), gate_tile, pl.BlockSpec(memory_space=pl.ANY)]
    args = [eidx, eidx, h, g.reshape(1, D), gates_t, tab]
    out_specs, out_shape = tile, jax.ShapeDtypeStruct((T, D), F32)
    if staged is not None:
        in_specs += [tile, gate_tile, pl.BlockSpec((tt * NPICK, D), lambda i: (i, 0))]
        args += list(staged)
        out_specs, out_shape = [tile, tile], [out_shape, out_shape]
    nslots = unroll * (2 if staged is not None else 1)
    return pl.pallas_call(
        functools.partial(_peer_expert_body, tt=tt, nsteps=nsteps, unroll=unroll, staged=staged is not None),
        grid=(nsteps,),
        in_specs=in_specs,
        out_specs=out_specs,
        out_shape=out_shape,
        scratch_shapes=[
            pltpu.VMEM((tt * NPICK * SUB, LANE), jnp.int32),
            pltpu.VMEM((tt * NPICK * SUB, LANE), jnp.int32),
            pltpu.VMEM((nslots, SUB, NPICK, LANE), F32),
            pltpu.SemaphoreType.DMA((2,)),
        ],
        compiler_params=_params("arbitrary"),
        name="peer_expert_mixed" if staged is not None else "peer_expert",
    )(*args)


SC_ROWS = 32
SC_IDX = 128


def sc_gather_rows(tab, idx):
    _, W = tab.shape
    N = idx.shape[0]
    info = plsc.get_sparse_core_info()
    nw = info.num_cores * info.num_subcores
    per_w = N // nw
    assert per_w * nw == N and per_w % SC_IDX == 0
    mesh = plsc.VectorSubcoreMesh(core_axis_name="core", subcore_axis_name="subcore")

    @functools.partial(
        pl.kernel, out_type=jax.ShapeDtypeStruct((N, W), tab.dtype), mesh=mesh,
        scratch_types=[pltpu.VMEM((SC_IDX,), jnp.int32), pltpu.VMEM((SC_ROWS, W), tab.dtype),
                       pltpu.VMEM((SC_ROWS, W), tab.dtype), pltpu.SemaphoreType.DMA, pltpu.SemaphoreType.DMA,
                       pltpu.SemaphoreType.DMA])
    def gather(tab_hbm, idx_hbm, out_hbm, idx_v, rows0, rows1, gsem, wsem0, wsem1):
        wid = lax.axis_index("subcore") * info.num_cores + lax.axis_index("core")
        base = wid * per_w
        bufs = ((rows0, wsem0), (rows1, wsem1))

        @pl.loop(0, per_w // SC_IDX)
        def _(c):
            off = base + c * SC_IDX
            pltpu.sync_copy(idx_hbm.at[pl.ds(off, SC_IDX)], idx_v)
            for j in range(SC_IDX // SC_ROWS):
                buf, wsem = bufs[j % 2]
                dst = out_hbm.at[pl.ds(off + j * SC_ROWS, SC_ROWS)]

                @pl.when((c > 0) | (j >= 2))
                def _():
                    pltpu.make_async_copy(buf, dst, wsem).wait()

                pltpu.async_copy(tab_hbm.at[idx_v.at[pl.ds(j * SC_ROWS, SC_ROWS)]], buf, gsem).wait()
                pltpu.async_copy(buf, dst, wsem)

        for buf, wsem in bufs:
            pltpu.make_async_copy(buf, out_hbm.at[pl.ds(base, SC_ROWS)], wsem).wait()

    return gather(tab, idx)


PEER_LEAD = 2048
PEER_CALLS = 5


def _tile_gates(gates_t, tt):
    npick, T = gates_t.shape
    return gates_t.reshape(npick, T // tt, tt).transpose(1, 0, 2)


def peer_ffn(h, g, w_q, sub_keys, u_tab, v_tab, *, tt=16):
    T, D = h.shape
    eidx_t, gates_t = peer_route(h, g, w_q, sub_keys)
    eidx = eidx_t.T
    tab = pack_uv(u_tab, v_tab)
    tab2d = tab.reshape(tab.shape[0], D)
    half = (T - PEER_LEAD) // (2 * PEER_CALLS)
    assert PEER_LEAD + 2 * PEER_CALLS * half == T and half % 32 == 0 and PEER_LEAD % tt == 0
    gates = lambda lo, hi: _tile_gates(gates_t[:, lo:hi], tt)
    outs = [peer_expert(h[:PEER_LEAD], g, eidx[:PEER_LEAD], gates(0, PEER_LEAD), tab, tt=tt)]
    for c in range(PEER_CALLS):
        lo = PEER_LEAD + 2 * c * half
        mid, hi = lo + half, lo + 2 * half
        rows = sc_gather_rows(tab2d, eidx[mid:hi].reshape(-1))
        outs += peer_expert(h[lo:mid], g, eidx[lo:mid], gates(lo, mid), tab,
                            staged=(h[mid:hi], gates(mid, hi), rows), tt=tt)
    return jnp.concatenate(outs, axis=0)


def _ple_body(h_ref, p_ref, g_ref, wg_ref, wp_ref, gf_ref, o_ref, *, final):
    x = h_ref[...]
    gate = jax.nn.sigmoid(jnp.dot(_rms(x, g_ref[...]).astype(BF16), wg_ref[...], preferred_element_type=F32))
    emb = jnp.dot(p_ref[...].astype(BF16), wp_ref[...], preferred_element_type=F32)
    y = x + gate * emb
    o_ref[...] = _rms(y, gf_ref[...]) if final else y


def per_layer_embed(h, p, g, w_gate, w_proj, g_final, *, final, tm=512):
    T, D = h.shape
    P = p.shape[1]
    tile = pl.BlockSpec((tm, D), lambda i: (i, 0))
    return pl.pallas_call(
        functools.partial(_ple_body, final=final),
        grid=(T // tm,),
        in_specs=[tile, pl.BlockSpec((tm, P), lambda i: (i, 0)), _resident((1, D)), _resident((D, D)),
                  _resident((P, D)), _resident((1, D))],
        out_specs=tile,
        out_shape=jax.ShapeDtypeStruct((T, D), F32),
        compiler_params=_params("parallel"),
        name="per_layer_embed",
    )(h, p, g.reshape(1, D), w_gate.astype(BF16), w_proj.astype(BF16), g_final.reshape(1, D))


def kernel(x, p, norm_mix, a_w_in, a_conv_w, a_conv_b, a_w_r, a_w_i, a_b_r, a_b_i, a_lambda,
           a_w_out, kv_norm, w_kv, b_w_q, b_w_o, norm_ffn, peer_w_q, peer_sub_keys, peer_u,
           peer_v, norm_ple, ple_w_gate, ple_w_proj, final_norm):
    B, S, D = x.shape
    T = B * S
    depth = norm_mix.shape[0]
    n_a = a_w_in.shape[0]
    h = x.reshape(T, D)
    k_sh = v_sh = None
    for i in range(depth):
        if i < n_a:
            j = i
            h = rglru_mixer(h, norm_mix[i], a_w_in[j], a_conv_w[j], a_conv_b[j], a_w_r[j], a_w_i[j],
                            a_b_r[j], a_b_i[j], a_lambda[j], a_w_out[j], batch=B)
        else:
            j = i - n_a
            q, k_new, v_new = qkv_project(h, norm_mix[i], kv_norm, b_w_q[j], w_kv)
            if i == n_a:
                k_sh, v_sh = k_new, v_new
            o = stick_breaking_attention(q, k_sh, v_sh, batch=B)
            h = matmul_residual(h, o, b_w_o[j])
        h = peer_ffn(h, norm_ffn[i], peer_w_q[i], peer_sub_keys[i], peer_u[i], peer_v[i])
        h = per_layer_embed(h, p[i].reshape(T, -1), norm_ple[i], ple_w_gate[i], ple_w_proj[i], final_norm,
                            final=(i == depth - 1))
    return h.reshape(B, S, D)
```

```python
import functools

import jax
import jax.numpy as jnp
from jax import lax
from jax.experimental import pallas as pl
from jax.experimental.pallas import tpu as pltpu
from jax.experimental.pallas import tpu_sc as plsc

F32 = jnp.float32
BF16 = jnp.bfloat16

EPS = 1e-6
LRU_HEADS = 8
CONV_WIDTH = 4
LRU_C = 8.0
ATTN_HEADS = 16
ATTN_HEAD_DIM = 64
PEER_HEADS = 8
PEER_NKEYS = 128
PEER_TOPK = 16

SUB, LANE = 8, 128
HIGH_HALF = -65536
VMEM_LIMIT = 48 * 1024 * 1024


def _params(*semantics):
    return pltpu.CompilerParams(dimension_semantics=semantics, vmem_limit_bytes=VMEM_LIMIT)


def _rms(x, g):
    return x * lax.rsqrt(jnp.mean(x * x, axis=-1, keepdims=True) + EPS) * g


def _softplus(z):
    return jnp.maximum(z, 0.0) + jnp.log1p(jnp.exp(-jnp.abs(z)))


def _resident(shape):
    return pl.BlockSpec(shape, lambda *_: (0,) * len(shape))


def _rglru_body(h_ref, g_ref, win_ref, cw_ref, cb_ref, wr_ref, wi_ref, br_ref, bi_ref, lam_ref,
                wout_ref, o_ref, tail_ref, state_ref):
    ts, _ = h_ref.shape
    width = wout_ref.shape[0]
    blk = width // LRU_HEADS

    @pl.when(pl.program_id(1) == 0)
    def _():
        tail_ref[...] = jnp.zeros_like(tail_ref)
        state_ref[...] = jnp.zeros_like(state_ref)

    x = h_ref[...]
    proj = jnp.dot(_rms(x, g_ref[...]).astype(BF16), win_ref[...], preferred_element_type=F32)
    yb, xb = proj[:, :width], proj[:, width:]

    row = lax.broadcasted_iota(jnp.int32, (ts, width), 0)
    ext = jnp.concatenate([tail_ref[...], xb], axis=0)
    xc = cw_ref[CONV_WIDTH - 1:CONV_WIDTH, :] * xb + cb_ref[...]
    for d in range(1, CONV_WIDTH):
        xc = xc + cw_ref[CONV_WIDTH - 1 - d:CONV_WIDTH - d, :] * pltpu.roll(ext, d, axis=0)[SUB:, :]
    tail_ref[...] = xb[ts - SUB:, :]

    xcb = xc.astype(BF16)
    r = jnp.concatenate([jnp.dot(xcb[:, k * blk:(k + 1) * blk], wr_ref[k], preferred_element_type=F32)
                         for k in range(LRU_HEADS)], axis=1)
    ig = jnp.concatenate([jnp.dot(xcb[:, k * blk:(k + 1) * blk], wi_ref[k], preferred_element_type=F32)
                          for k in range(LRU_HEADS)], axis=1)
    r = jax.nn.sigmoid(r + br_ref[...])
    ig = jax.nn.sigmoid(ig + bi_ref[...])
    log_a = -LRU_C * r * _softplus(-lam_ref[...])
    a = jnp.exp(log_a)
    u = jnp.sqrt(-jnp.tanh(log_a) * (a * a + 1.0)) * ig * xc

    d = 1
    while d < ts:
        keep = row >= d
        a_prev = jnp.where(keep, pltpu.roll(a, d, axis=0), 1.0)
        u_prev = jnp.where(keep, pltpu.roll(u, d, axis=0), 0.0)
        u = a * u_prev + u
        a = a * a_prev
        d *= 2
    hseq = u + a * state_ref[SUB - 1:SUB, :]
    state_ref[...] = hseq[ts - SUB:, :]

    gated = (jax.nn.gelu(yb) * hseq).astype(BF16)
    o_ref[...] = x + jnp.dot(gated, wout_ref[...], preferred_element_type=F32)


def rglru_mixer(h, g, w_in, conv_w, conv_b, w_r, w_i, b_r, b_i, lam, w_out, *, batch, ts=256):
    T, D = h.shape
    S = T // batch
    nt = S // ts
    W = w_out.shape[0]
    tile = pl.BlockSpec((ts, D), lambda b, j: (b * nt + j, 0))
    vec = lambda a: a.reshape(1, -1)
    return pl.pallas_call(
        _rglru_body,
        grid=(batch, nt),
        in_specs=[tile, _resident((1, D)), _resident((D, 2 * W)), _resident((CONV_WIDTH, W)),
                  _resident((1, W)), _resident(w_r.shape), _resident(w_i.shape), _resident((1, W)),
                  _resident((1, W)), _resident((1, W)), _resident((W, D))],
        out_specs=tile,
        out_shape=jax.ShapeDtypeStruct((T, D), F32),
        scratch_shapes=[pltpu.VMEM((SUB, W), F32), pltpu.VMEM((SUB, W), F32)],
        compiler_params=_params("parallel", "arbitrary"),
        name="rglru_mixer",
    )(h, vec(g), w_in.astype(BF16), conv_w, vec(conv_b), w_r.astype(BF16), w_i.astype(BF16),
      vec(b_r), vec(b_i), vec(lam), w_out.astype(BF16))


def _qkv_body(h_ref, gq_ref, gkv_ref, wq_ref, wkv_ref, q_ref, k_ref, v_ref, *, scale):
    x = h_ref[...]
    xhat = x * lax.rsqrt(jnp.mean(x * x, axis=-1, keepdims=True) + EPS)
    q = jnp.dot((xhat * gq_ref[...]).astype(BF16), wq_ref[...], preferred_element_type=F32)
    kv = jnp.dot((xhat * gkv_ref[...]).astype(BF16), wkv_ref[...], preferred_element_type=F32)
    aw = k_ref.shape[-1]
    q_ref[...] = (q * scale).astype(q_ref.dtype)
    k_ref[...] = kv[:, :aw].astype(k_ref.dtype)
    v_ref[...] = kv[:, aw:].astype(v_ref.dtype)


def qkv_project(h, g_q, g_kv, w_q, w_kv, *, tm=512):
    T, D = h.shape
    aw = w_q.shape[1]
    tile = pl.BlockSpec((tm, D), lambda i: (i, 0))
    out = pl.BlockSpec((tm, aw), lambda i: (i, 0))
    return pl.pallas_call(
        functools.partial(_qkv_body, scale=ATTN_HEAD_DIM ** -0.5),
        grid=(T // tm,),
        in_specs=[tile, _resident((1, D)), _resident((1, D)), _resident((D, aw)), _resident((D, 2 * aw))],
        out_specs=[out, out, out],
        out_shape=[jax.ShapeDtypeStruct((T, aw), BF16)] * 3,
        compiler_params=_params("parallel"),
        name="qkv_project",
    )(h, g_q.reshape(1, D), g_kv.reshape(1, D), w_q.astype(BF16), w_kv.astype(BF16))


ATTN_TQ, ATTN_TK = 512, 256


def _stick_body(q_ref, k_ref, v_ref, o_ref, *, tq, tk, dh):
    S, lanes = q_ref.shape
    nh = lanes // dh
    ndiag = tq // tk
    row = lax.broadcasted_iota(jnp.int32, (tk, tk), 0)
    col = lax.broadcasted_iota(jnp.int32, (tk, tk), 1)
    neg_from = jnp.where(row >= col, -1.0, 0.0).astype(BF16)
    qpos = lax.broadcasted_iota(jnp.int32, (tq, tk), 0)
    kpos = lax.broadcasted_iota(jnp.int32, (tq, tk), 1)
    head_of_lane = lax.broadcasted_iota(jnp.int32, (tq, lanes), 1) // dh

    def pair(q, k, v, carry, shift):
        acc, right = carry
        z = lax.dot_general(q, k, (((1,), (1,)), ((), ())), preferred_element_type=F32)
        sp = jnp.maximum(z, 0.0) + jnp.log(1.0 + jnp.exp(-jnp.abs(z)))
        if shift is not None:
            causal = kpos < qpos + shift
            sp = jnp.where(causal, sp, 0.0)
        hi = pltpu.bitcast(pltpu.bitcast(sp, jnp.int32) & HIGH_HALF, F32)
        lo = sp - hi
        tail = (jnp.dot(hi.astype(BF16), neg_from, preferred_element_type=F32)
                + jnp.dot(lo.astype(BF16), neg_from, preferred_element_type=F32))
        a = jnp.exp(z + tail + right)
        if shift is not None:
            a = jnp.where(causal, a, 0.0)
        acc = acc + jnp.dot(a.astype(BF16), v, preferred_element_type=F32)
        return acc, right + tail[:, 0:1]

    def pairs(qs, kb, carries, shift):
        k = k_ref[pl.ds(pl.multiple_of(kb * tk, tk), tk), :]
        v = v_ref[pl.ds(pl.multiple_of(kb * tk, tk), tk), :]
        return tuple(pair(qs[h], k, v, carries[h], shift) for h in range(nh))

    def q_block(qi, c):
        q = q_ref[pl.ds(pl.multiple_of(qi * tq, tq), tq), :]
        qs = [jnp.where(head_of_lane == h, q, jnp.zeros_like(q)) for h in range(nh)]
        carries = tuple((jnp.zeros((tq, lanes), F32), jnp.zeros((tq, 1), F32)) for _ in range(nh))
        last = (qi + 1) * ndiag - 1
        for d in range(ndiag):
            carries = pairs(qs, last - d, carries, (d + 1 - ndiag) * tk)

        def past(n, cr):
            for d in range(ndiag):
                cr = pairs(qs, (qi - n) * ndiag - 1 - d, cr, None)
            return cr

        carries = lax.fori_loop(0, qi, past, carries)
        out = carries[0][0]
        for h in range(1, nh):
            out = jnp.where(head_of_lane == h, carries[h][0], out)
        o_ref[pl.ds(pl.multiple_of(qi * tq, tq), tq), :] = out.astype(o_ref.dtype)
        return c

    lax.fori_loop(0, S // tq, q_block, 0)


def stick_breaking_attention(q, k, v, *, batch, tq=ATTN_TQ, tk=ATTN_TK):
    T, width = q.shape
    S = T // batch
    tq = min(tq, S)
    tk = min(tk, tq)
    spec = pl.BlockSpec((S, LANE), lambda b, g: (b, g))
    return pl.pallas_call(
        functools.partial(_stick_body, tq=tq, tk=tk, dh=ATTN_HEAD_DIM),
        grid=(batch, width // LANE),
        in_specs=[spec, spec, spec],
        out_specs=spec,
        out_shape=jax.ShapeDtypeStruct((T, width), BF16),
        compiler_params=_params("parallel", "parallel"),
        name="stick_breaking",
    )(q, k, v)


def _matmul_residual_body(h_ref, x_ref, w_ref, o_ref):
    o_ref[...] = h_ref[...] + jnp.dot(x_ref[...], w_ref[...], preferred_element_type=F32)


def matmul_residual(h, x, w, *, tm=512):
    T, D = h.shape
    K = x.shape[1]
    return pl.pallas_call(
        _matmul_residual_body,
        grid=(T // tm,),
        in_specs=[pl.BlockSpec((tm, D), lambda i: (i, 0)), pl.BlockSpec((tm, K), lambda i: (i, 0)),
                  _resident((K, D))],
        out_specs=pl.BlockSpec((tm, D), lambda i: (i, 0)),
        out_shape=jax.ShapeDtypeStruct((T, D), F32),
        compiler_params=_params("parallel"),
        name="matmul_residual",
    )(h, x, w.astype(BF16))


def _topk_rows(s, k, payload=None):
    n = s.shape[0]
    rows = lax.broadcasted_iota(jnp.int32, s.shape, 0)
    vals, sel = [], []
    for _ in range(k):
        best = jnp.max(s, axis=0, keepdims=True)
        pos = jnp.min(jnp.where(s == best, rows, n), axis=0, keepdims=True)
        hit = rows == pos
        vals.append(best)
        sel.append(pos if payload is None else jnp.max(jnp.where(hit, payload, -1), axis=0, keepdims=True))
        s = jnp.where(hit, -jnp.inf, s)
    return jnp.concatenate(vals, axis=0), jnp.concatenate(sel, axis=0)


def _peer_route_body(h_ref, g_ref, wqt_ref, keys_ref, eidx_ref, gates_ref):
    xn = _rms(h_ref[...], g_ref[...]).astype(BF16)
    qt = lax.dot_general(wqt_ref[...], xn, (((1,), (1,)), ((), ())), preferred_element_type=F32)
    half = keys_ref.shape[-1]
    k = PEER_TOPK
    for hd in range(PEER_HEADS):
        sv, si = [], []
        for part in range(2):
            hp = hd * 2 + part
            q = qt[hp * half:(hp + 1) * half, :].astype(BF16)
            scores = jnp.dot(keys_ref[hp], q, preferred_element_type=F32)
            v, i = _topk_rows(scores, k)
            sv.append(v)
            si.append(i)
        widths = [k // (a + 1) for a in range(k)]
        pad = -sum(widths) % SUB
        cand = jnp.concatenate([sv[0][a:a + 1, :] + sv[1][:w, :] for a, w in enumerate(widths)]
                               + [jnp.full((pad, sv[0].shape[1]), -jnp.inf, F32)], axis=0)
        pay = jnp.concatenate([si[0][a:a + 1, :] * PEER_NKEYS + si[1][:w, :] for a, w in enumerate(widths)]
                              + [jnp.zeros((pad, sv[0].shape[1]), jnp.int32)], axis=0)
        cv, ce = _topk_rows(cand, k, payload=pay)
        ex = jnp.exp(cv - cv[0:1, :])
        gates_ref[hd * k:(hd + 1) * k, :] = ex / jnp.sum(ex, axis=0, keepdims=True)
        eidx_ref[hd * k:(hd + 1) * k, :] = ce


def peer_route(h, g, w_q, sub_keys, *, tm=256):
    T, D = h.shape
    H, _, nkeys, half = sub_keys.shape
    npick = H * PEER_TOPK
    return pl.pallas_call(
        _peer_route_body,
        grid=(T // tm,),
        in_specs=[pl.BlockSpec((tm, D), lambda i: (i, 0)), _resident((1, D)),
                  _resident((H * 2 * half, D)), _resident((H * 2, nkeys, half))],
        out_specs=[pl.BlockSpec((npick, tm), lambda i: (0, i)),
                   pl.BlockSpec((npick, tm), lambda i: (0, i))],
        out_shape=[jax.ShapeDtypeStruct((npick, T), jnp.int32),
                   jax.ShapeDtypeStruct((npick, T), F32)],
        compiler_params=_params("parallel"),
        name="peer_route",
    )(h, g.reshape(1, D), w_q.T.astype(BF16), sub_keys.reshape(H * 2, nkeys, half).astype(BF16))


NPICK = PEER_HEADS * PEER_TOPK


def _expert_token(t, words_of_chunk, vs, h_ref, g_ref, gates_ref, o_ref, tok_lane):
    x = h_ref[pl.ds(t, 1), :]
    xn = _rms(x, g_ref[...])
    r = jnp.zeros((NPICK, LANE), F32)
    for c in range(SUB):
        words = words_of_chunk(c)
        u = pltpu.bitcast(words << 16, F32)
        vs[c] = pltpu.bitcast(words & HIGH_HALF, F32)
        r = r + u * xn[:, c * LANE:(c + 1) * LANE]
    dots = jnp.sum(r, axis=1, keepdims=True)
    gcol = jnp.sum(jnp.where(tok_lane == t, gates_ref[...], 0.0), axis=1, keepdims=True)
    coef = gcol * jax.nn.gelu(dots)
    ffn = jnp.concatenate(
        [jnp.sum(coef * vs[c], axis=0, keepdims=True) for c in range(SUB)], axis=1)
    o_ref[pl.ds(t, 1), :] = x + ffn


def _peer_expert_body(*refs, tt, nsteps, unroll, staged):
    if staged:
        (idx_cur, idx_nxt, h_ref, g_ref, gates_ref, tab_ref, h2_ref, gates2_ref, rows_ref,
         o_ref, o2_ref, buf0, buf1, vscr, sem) = refs
    else:
        idx_cur, idx_nxt, h_ref, g_ref, gates_ref, tab_ref, o_ref, buf0, buf1, vscr, sem = refs
    i = pl.program_id(0)
    tok_lane = lax.broadcasted_iota(jnp.int32, (NPICK, tt), 1)

    def issue_token(idx_ref, t, dst, dsem):
        for k in range(NPICK):
            row = pl.multiple_of((t * NPICK + k) * SUB, SUB)
            pltpu.make_async_copy(tab_ref.at[idx_ref[t, k]], dst.at[pl.ds(row, SUB), :],
                                  dsem).start(priority=k % 2)

    def compute_token(t, src, vs):
        base = pl.multiple_of(t * (NPICK * SUB), SUB)
        _expert_token(t, lambda c: src[pl.ds(base + c, NPICK, stride=SUB), :], vs,
                      h_ref, g_ref, gates_ref, o_ref, tok_lane)

    def compute_staged(t, vs):
        base = pl.multiple_of(t * (NPICK * SUB), SUB)
        _expert_token(t, lambda c: rows_ref[pl.ds(base + c, NPICK, stride=SUB), :], vs,
                      h2_ref, g_ref, gates2_ref, o2_ref, tok_lane)

    def step(src, ssem, dst, dsem, first):
        if first:
            @pl.when(i == 0)
            def _():
                def prologue(t, c):
                    issue_token(idx_cur, t, src, ssem)
                    return c
                lax.fori_loop(0, tt, prologue, 0)

        pltpu.make_async_copy(dst, src, ssem).wait()

        def tokens(with_issue):
            def body(j, c):
                for r in range(unroll):
                    t = j * unroll + r
                    if with_issue:
                        issue_token(idx_nxt, t, dst, dsem)
                    compute_token(t, src, vscr.at[r])
                    if staged:
                        compute_staged(t, vscr.at[unroll + r])
                return c
            lax.fori_loop(0, tt // unroll, body, 0)

        @pl.when(i + 1 < nsteps)
        def _():
            tokens(True)

        @pl.when(i + 1 == nsteps)
        def _():
            tokens(False)

    @pl.when(i % 2 == 0)
    def _():
        step(buf0, sem.at[0], buf1, sem.at[1], True)

    @pl.when(i % 2 == 1)
    def _():
        step(buf1, sem.at[1], buf0, sem.at[0], False)


def pack_uv(u, v):
    E, D = u.shape
    ub = lax.bitcast_convert_type(u.astype(BF16), jnp.uint16).astype(jnp.uint32)
    vb = lax.bitcast_convert_type(v.astype(BF16), jnp.uint16).astype(jnp.uint32)
    return lax.bitcast_convert_type(ub | (vb << 16), jnp.int32).reshape(E, SUB, D // SUB)


def peer_expert(h, g, eidx, gates_t, tab, staged=None, *, tt=16, unroll=2):
    T, D = h.shape
    nsteps = T // tt
    tile = pl.BlockSpec((tt, D), lambda i: (i, 0))
    gate_tile = pl.BlockSpec((None, NPICK, tt), lambda i: (i, 0, 0))
    in_specs = [
        pl.BlockSpec((tt, NPICK), lambda i: (i, 0), memory_space=pltpu.SMEM),
        pl.BlockSpec((tt, NPICK), lambda i: (jnp.minimum(i + 1, nsteps - 1), 0), memory_space=pltpu.SMEM),
        tile, _resident((1, D)), gate_tile, pl.BlockSpec(memory_space=pl.ANY)]
    args = [eidx, eidx, h, g.reshape(1, D), gates_t, tab]
    out_specs, out_shape = tile, jax.ShapeDtypeStruct((T, D), F32)
    if staged is not None:
        in_specs += [tile, gate_tile, pl.BlockSpec((tt * NPICK * SUB, LANE), lambda i: (i, 0))]
        args += list(staged)
        out_specs, out_shape = [tile, tile], [out_shape, out_shape]
    nslots = unroll * (2 if staged is not None else 1)
    return pl.pallas_call(
        functools.partial(_peer_expert_body, tt=tt, nsteps=nsteps, unroll=unroll, staged=staged is not None),
        grid=(nsteps,),
        in_specs=in_specs,
        out_specs=out_specs,
        out_shape=out_shape,
        scratch_shapes=[
            pltpu.VMEM((tt * NPICK * SUB, LANE), jnp.int32),
            pltpu.VMEM((tt * NPICK * SUB, LANE), jnp.int32),
            pltpu.VMEM((nslots, SUB, NPICK, LANE), F32),
            pltpu.SemaphoreType.DMA((2,)),
        ],
        compiler_params=_params("arbitrary"),
        name="peer_expert_mixed" if staged is not None else "peer_expert",
    )(*args)


SC_ROWS = 32
SC_IDX = 128


def sc_gather_rows(tab, idx):
    row = tab.shape[1:]
    N = idx.shape[0]
    info = plsc.get_sparse_core_info()
    nw = info.num_cores * info.num_subcores
    per_w = N // nw
    assert per_w * nw == N and per_w % SC_IDX == 0
    mesh = plsc.VectorSubcoreMesh(core_axis_name="core", subcore_axis_name="subcore")

    @functools.partial(
        pl.kernel, out_type=jax.ShapeDtypeStruct((N,) + row, tab.dtype), mesh=mesh,
        scratch_types=[pltpu.VMEM((SC_IDX,), jnp.int32), pltpu.VMEM((SC_ROWS,) + row, tab.dtype),
                       pltpu.VMEM((SC_ROWS,) + row, tab.dtype), pltpu.SemaphoreType.DMA, pltpu.SemaphoreType.DMA,
                       pltpu.SemaphoreType.DMA])
    def gather(tab_hbm, idx_hbm, out_hbm, idx_v, rows0, rows1, gsem, wsem0, wsem1):
        wid = lax.axis_index("subcore") * info.num_cores + lax.axis_index("core")
        base = wid * per_w
        bufs = ((rows0, wsem0), (rows1, wsem1))

        @pl.loop(0, per_w // SC_IDX)
        def _(c):
            off = base + c * SC_IDX
            pltpu.sync_copy(idx_hbm.at[pl.ds(off, SC_IDX)], idx_v)
            for j in range(SC_IDX // SC_ROWS):
                buf, wsem = bufs[j % 2]
                dst = out_hbm.at[pl.ds(off + j * SC_ROWS, SC_ROWS)]

                @pl.when((c > 0) | (j >= 2))
                def _():
                    pltpu.make_async_copy(buf, dst, wsem).wait()

                pltpu.async_copy(tab_hbm.at[idx_v.at[pl.ds(j * SC_ROWS, SC_ROWS)]], buf, gsem).wait()
                pltpu.async_copy(buf, dst, wsem)

        for buf, wsem in bufs:
            pltpu.make_async_copy(buf, out_hbm.at[pl.ds(base, SC_ROWS)], wsem).wait()

    return gather(tab, idx)


PEER_LEAD = 2048
PEER_CALLS = 5


def _tile_gates(gates_t, tt):
    npick, T = gates_t.shape
    return gates_t.reshape(npick, T // tt, tt).transpose(1, 0, 2)


def peer_ffn(h, g, w_q, sub_keys, u_tab, v_tab, *, tt=16):
    T, D = h.shape
    eidx_t, gates_t = peer_route(h, g, w_q, sub_keys)
    eidx = eidx_t.T
    tab = pack_uv(u_tab, v_tab)
    half = (T - PEER_LEAD) // (2 * PEER_CALLS)
    assert PEER_LEAD + 2 * PEER_CALLS * half == T and half % 32 == 0 and PEER_LEAD % tt == 0
    gates = lambda lo, hi: _tile_gates(gates_t[:, lo:hi], tt)
    outs = [peer_expert(h[:PEER_LEAD], g, eidx[:PEER_LEAD], gates(0, PEER_LEAD), tab, tt=tt)]
    for c in range(PEER_CALLS):
        lo = PEER_LEAD + 2 * c * half
        mid, hi = lo + half, lo + 2 * half
        rows = sc_gather_rows(tab, eidx[mid:hi].reshape(-1)).reshape(-1, LANE)
        outs += peer_expert(h[lo:mid], g, eidx[lo:mid], gates(lo, mid), tab,
                            staged=(h[mid:hi], gates(mid, hi), rows), tt=tt)
    return jnp.concatenate(outs, axis=0)


def _ple_body(h_ref, p_ref, g_ref, wg_ref, wp_ref, gf_ref, o_ref, *, final):
    x = h_ref[...]
    gate = jax.nn.sigmoid(jnp.dot(_rms(x, g_ref[...]).astype(BF16), wg_ref[...], preferred_element_type=F32))
    emb = jnp.dot(p_ref[...].astype(BF16), wp_ref[...], preferred_element_type=F32)
    y = x + gate * emb
    o_ref[...] = _rms(y, gf_ref[...]) if final else y


def per_layer_embed(h, p, g, w_gate, w_proj, g_final, *, final, tm=512):
    T, D = h.shape
    P = p.shape[1]
    tile = pl.BlockSpec((tm, D), lambda i: (i, 0))
    return pl.pallas_call(
        functools.partial(_ple_body, final=final),
        grid=(T // tm,),
        in_specs=[tile, pl.BlockSpec((tm, P), lambda i: (i, 0)), _resident((1, D)), _resident((D, D)),
                  _resident((P, D)), _resident((1, D))],
        out_specs=tile,
        out_shape=jax.ShapeDtypeStruct((T, D), F32),
        compiler_params=_params("parallel"),
        name="per_layer_embed",
    )(h, p, g.reshape(1, D), w_gate.astype(BF16), w_proj.astype(BF16), g_final.reshape(1, D))


def kernel(x, p, norm_mix, a_w_in, a_conv_w, a_conv_b, a_w_r, a_w_i, a_b_r, a_b_i, a_lambda,
           a_w_out, kv_norm, w_kv, b_w_q, b_w_o, norm_ffn, peer_w_q, peer_sub_keys, peer_u,
           peer_v, norm_ple, ple_w_gate, ple_w_proj, final_norm):
    B, S, D = x.shape
    T = B * S
    depth = norm_mix.shape[0]
    n_a = a_w_in.shape[0]
    h = x.reshape(T, D)
    k_sh = v_sh = None
    for i in range(depth):
        if i < n_a:
            j = i
            h = rglru_mixer(h, norm_mix[i], a_w_in[j], a_conv_w[j], a_conv_b[j], a_w_r[j], a_w_i[j],
                            a_b_r[j], a_b_i[j], a_lambda[j], a_w_out[j], batch=B)
        else:
            j = i - n_a
            q, k_new, v_new = qkv_project(h, norm_mix[i], kv_norm, b_w_q[j], w_kv)
            if i == n_a:
                k_sh, v_sh = k_new, v_new
            o = stick_breaking_attention(q, k_sh, v_sh, batch=B)
            h = matmul_residual(h, o, b_w_o[j])
        h = peer_ffn(h, norm_ffn[i], peer_w_q[i], peer_sub_keys[i], peer_u[i], peer_v[i])
        h = per_layer_embed(h, p[i].reshape(T, -1), norm_ple[i], ple_w_gate[i], ple_w_proj[i], final_norm,
                            final=(i == depth - 1))
    return h.reshape(B, S, D)
```

```python
import functools

import jax
import jax.numpy as jnp
from jax import lax
from jax.experimental import pallas as pl
from jax.experimental.pallas import tpu as pltpu
from jax.experimental.pallas import tpu_sc as plsc

F32 = jnp.float32
BF16 = jnp.bfloat16

EPS = 1e-6
LRU_HEADS = 8
CONV_WIDTH = 4
LRU_C = 8.0
ATTN_HEADS = 16
ATTN_HEAD_DIM = 64
PEER_HEADS = 8
PEER_NKEYS = 128
PEER_TOPK = 16

SUB, LANE = 8, 128
HIGH_HALF = -65536
VMEM_LIMIT = 48 * 1024 * 1024


def _params(*semantics):
    return pltpu.CompilerParams(dimension_semantics=semantics, vmem_limit_bytes=VMEM_LIMIT)


def _rms(x, g):
    return x * lax.rsqrt(jnp.mean(x * x, axis=-1, keepdims=True) + EPS) * g


def _softplus(z):
    return jnp.maximum(z, 0.0) + jnp.log1p(jnp.exp(-jnp.abs(z)))


def _resident(shape):
    return pl.BlockSpec(shape, lambda *_: (0,) * len(shape))


def _rglru_body(h_ref, g_ref, win_ref, cw_ref, cb_ref, wr_ref, wi_ref, br_ref, bi_ref, lam_ref,
                wout_ref, o_ref, tail_ref, state_ref):
    ts, _ = h_ref.shape
    width = wout_ref.shape[0]
    blk = width // LRU_HEADS

    @pl.when(pl.program_id(1) == 0)
    def _():
        tail_ref[...] = jnp.zeros_like(tail_ref)
        state_ref[...] = jnp.zeros_like(state_ref)

    x = h_ref[...]
    proj = jnp.dot(_rms(x, g_ref[...]).astype(BF16), win_ref[...], preferred_element_type=F32)
    yb, xb = proj[:, :width], proj[:, width:]

    row = lax.broadcasted_iota(jnp.int32, (ts, width), 0)
    ext = jnp.concatenate([tail_ref[...], xb], axis=0)
    xc = cw_ref[CONV_WIDTH - 1:CONV_WIDTH, :] * xb + cb_ref[...]
    for d in range(1, CONV_WIDTH):
        xc = xc + cw_ref[CONV_WIDTH - 1 - d:CONV_WIDTH - d, :] * pltpu.roll(ext, d, axis=0)[SUB:, :]
    tail_ref[...] = xb[ts - SUB:, :]

    xcb = xc.astype(BF16)
    r = jnp.concatenate([jnp.dot(xcb[:, k * blk:(k + 1) * blk], wr_ref[k], preferred_element_type=F32)
                         for k in range(LRU_HEADS)], axis=1)
    ig = jnp.concatenate([jnp.dot(xcb[:, k * blk:(k + 1) * blk], wi_ref[k], preferred_element_type=F32)
                          for k in range(LRU_HEADS)], axis=1)
    r = jax.nn.sigmoid(r + br_ref[...])
    ig = jax.nn.sigmoid(ig + bi_ref[...])
    log_a = -LRU_C * r * _softplus(-lam_ref[...])
    a = jnp.exp(log_a)
    u = jnp.sqrt(-jnp.tanh(log_a) * (a * a + 1.0)) * ig * xc

    d = 1
    while d < ts:
        keep = row >= d
        a_prev = jnp.where(keep, pltpu.roll(a, d, axis=0), 1.0)
        u_prev = jnp.where(keep, pltpu.roll(u, d, axis=0), 0.0)
        u = a * u_prev + u
        a = a * a_prev
        d *= 2
    hseq = u + a * state_ref[SUB - 1:SUB, :]
    state_ref[...] = hseq[ts - SUB:, :]

    gated = (jax.nn.gelu(yb) * hseq).astype(BF16)
    o_ref[...] = x + jnp.dot(gated, wout_ref[...], preferred_element_type=F32)


def rglru_mixer(h, g, w_in, conv_w, conv_b, w_r, w_i, b_r, b_i, lam, w_out, *, batch, ts=256):
    T, D = h.shape
    S = T // batch
    nt = S // ts
    W = w_out.shape[0]
    tile = pl.BlockSpec((ts, D), lambda b, j: (b * nt + j, 0))
    vec = lambda a: a.reshape(1, -1)
    return pl.pallas_call(
        _rglru_body,
        grid=(batch, nt),
        in_specs=[tile, _resident((1, D)), _resident((D, 2 * W)), _resident((CONV_WIDTH, W)),
                  _resident((1, W)), _resident(w_r.shape), _resident(w_i.shape), _resident((1, W)),
                  _resident((1, W)), _resident((1, W)), _resident((W, D))],
        out_specs=tile,
        out_shape=jax.ShapeDtypeStruct((T, D), F32),
        scratch_shapes=[pltpu.VMEM((SUB, W), F32), pltpu.VMEM((SUB, W), F32)],
        compiler_params=_params("parallel", "arbitrary"),
        name="rglru_mixer",
    )(h, vec(g), w_in.astype(BF16), conv_w, vec(conv_b), w_r.astype(BF16), w_i.astype(BF16),
      vec(b_r), vec(b_i), vec(lam), w_out.astype(BF16))


def _qkv_body(h_ref, gq_ref, gkv_ref, wq_ref, wkv_ref, q_ref, k_ref, v_ref, *, scale):
    x = h_ref[...]
    xhat = x * lax.rsqrt(jnp.mean(x * x, axis=-1, keepdims=True) + EPS)
    q = jnp.dot((xhat * gq_ref[...]).astype(BF16), wq_ref[...], preferred_element_type=F32)
    kv = jnp.dot((xhat * gkv_ref[...]).astype(BF16), wkv_ref[...], preferred_element_type=F32)
    aw = k_ref.shape[-1]
    q_ref[...] = (q * scale).astype(q_ref.dtype)
    k_ref[...] = kv[:, :aw].astype(k_ref.dtype)
    v_ref[...] = kv[:, aw:].astype(v_ref.dtype)


def qkv_project(h, g_q, g_kv, w_q, w_kv, *, tm=512):
    T, D = h.shape
    aw = w_q.shape[1]
    tile = pl.BlockSpec((tm, D), lambda i: (i, 0))
    out = pl.BlockSpec((tm, aw), lambda i: (i, 0))
    return pl.pallas_call(
        functools.partial(_qkv_body, scale=ATTN_HEAD_DIM ** -0.5),
        grid=(T // tm,),
        in_specs=[tile, _resident((1, D)), _resident((1, D)), _resident((D, aw)), _resident((D, 2 * aw))],
        out_specs=[out, out, out],
        out_shape=[jax.ShapeDtypeStruct((T, aw), BF16)] * 3,
        compiler_params=_params("parallel"),
        name="qkv_project",
    )(h, g_q.reshape(1, D), g_kv.reshape(1, D), w_q.astype(BF16), w_kv.astype(BF16))


ATTN_TQ, ATTN_TK = 512, 256


def _stick_body(q_ref, k_ref, v_ref, o_ref, *, tq, tk, dh):
    S, lanes = q_ref.shape
    nh = lanes // dh
    ndiag = tq // tk
    row = lax.broadcasted_iota(jnp.int32, (tk, tk), 0)
    col = lax.broadcasted_iota(jnp.int32, (tk, tk), 1)
    neg_from = jnp.where(row >= col, -1.0, 0.0).astype(BF16)
    qpos = lax.broadcasted_iota(jnp.int32, (tq, tk), 0)
    kpos = lax.broadcasted_iota(jnp.int32, (tq, tk), 1)
    head_of_lane = lax.broadcasted_iota(jnp.int32, (tq, lanes), 1) // dh

    def pair(q, k, v, carry, shift):
        acc, right = carry
        z = lax.dot_general(q, k, (((1,), (1,)), ((), ())), preferred_element_type=F32)
        sp = jnp.maximum(z, 0.0) + jnp.log(1.0 + jnp.exp(-jnp.abs(z)))
        if shift is not None:
            causal = kpos < qpos + shift
            sp = jnp.where(causal, sp, 0.0)
        hi = pltpu.bitcast(pltpu.bitcast(sp, jnp.int32) & HIGH_HALF, F32)
        lo = sp - hi
        tail = (jnp.dot(hi.astype(BF16), neg_from, preferred_element_type=F32)
                + jnp.dot(lo.astype(BF16), neg_from, preferred_element_type=F32))
        a = jnp.exp(z + tail + right)
        if shift is not None:
            a = jnp.where(causal, a, 0.0)
        acc = acc + jnp.dot(a.astype(BF16), v, preferred_element_type=F32)
        return acc, right + tail[:, 0:1]

    def pairs(qs, kb, carries, shift):
        k = k_ref[pl.ds(pl.multiple_of(kb * tk, tk), tk), :]
        v = v_ref[pl.ds(pl.multiple_of(kb * tk, tk), tk), :]
        return tuple(pair(qs[h], k, v, carries[h], shift) for h in range(nh))

    def q_block(qi, c):
        q = q_ref[pl.ds(pl.multiple_of(qi * tq, tq), tq), :]
        qs = [jnp.where(head_of_lane == h, q, jnp.zeros_like(q)) for h in range(nh)]
        carries = tuple((jnp.zeros((tq, lanes), F32), jnp.zeros((tq, 1), F32)) for _ in range(nh))
        last = (qi + 1) * ndiag - 1
        for d in range(ndiag):
            carries = pairs(qs, last - d, carries, (d + 1 - ndiag) * tk)

        def past(n, cr):
            for d in range(ndiag):
                cr = pairs(qs, (qi - n) * ndiag - 1 - d, cr, None)
            return cr

        carries = lax.fori_loop(0, qi, past, carries)
        out = carries[0][0]
        for h in range(1, nh):
            out = jnp.where(head_of_lane == h, carries[h][0], out)
        o_ref[pl.ds(pl.multiple_of(qi * tq, tq), tq), :] = out.astype(o_ref.dtype)
        return c

    lax.fori_loop(0, S // tq, q_block, 0)


def stick_breaking_attention(q, k, v, *, batch, tq=ATTN_TQ, tk=ATTN_TK):
    T, width = q.shape
    S = T // batch
    tq = min(tq, S)
    tk = min(tk, tq)
    spec = pl.BlockSpec((S, LANE), lambda b, g: (b, g))
    return pl.pallas_call(
        functools.partial(_stick_body, tq=tq, tk=tk, dh=ATTN_HEAD_DIM),
        grid=(batch, width // LANE),
        in_specs=[spec, spec, spec],
        out_specs=spec,
        out_shape=jax.ShapeDtypeStruct((T, width), BF16),
        compiler_params=_params("parallel", "parallel"),
        name="stick_breaking",
    )(q, k, v)


def _matmul_residual_body(h_ref, x_ref, w_ref, o_ref):
    o_ref[...] = h_ref[...] + jnp.dot(x_ref[...], w_ref[...], preferred_element_type=F32)


def matmul_residual(h, x, w, *, tm=512):
    T, D = h.shape
    K = x.shape[1]
    return pl.pallas_call(
        _matmul_residual_body,
        grid=(T // tm,),
        in_specs=[pl.BlockSpec((tm, D), lambda i: (i, 0)), pl.BlockSpec((tm, K), lambda i: (i, 0)),
                  _resident((K, D))],
        out_specs=pl.BlockSpec((tm, D), lambda i: (i, 0)),
        out_shape=jax.ShapeDtypeStruct((T, D), F32),
        compiler_params=_params("parallel"),
        name="matmul_residual",
    )(h, x, w.astype(BF16))


def _topk_rows(s, k, payload=None):
    n = s.shape[0]
    rows = lax.broadcasted_iota(jnp.int32, s.shape, 0)
    vals, sel = [], []
    for _ in range(k):
        best = jnp.max(s, axis=0, keepdims=True)
        pos = jnp.min(jnp.where(s == best, rows, n), axis=0, keepdims=True)
        hit = rows == pos
        vals.append(best)
        sel.append(pos if payload is None else jnp.max(jnp.where(hit, payload, -1), axis=0, keepdims=True))
        s = jnp.where(hit, -jnp.inf, s)
    return jnp.concatenate(vals, axis=0), jnp.concatenate(sel, axis=0)


def _peer_route_body(h_ref, g_ref, wqt_ref, keys_ref, eidx_ref, gates_ref):
    xn = _rms(h_ref[...], g_ref[...]).astype(BF16)
    qt = lax.dot_general(wqt_ref[...], xn, (((1,), (1,)), ((), ())), preferred_element_type=F32)
    half = keys_ref.shape[-1]
    k = PEER_TOPK
    for hd in range(PEER_HEADS):
        sv, si = [], []
        for part in range(2):
            hp = hd * 2 + part
            q = qt[hp * half:(hp + 1) * half, :].astype(BF16)
            scores = jnp.dot(keys_ref[hp], q, preferred_element_type=F32)
            v, i = _topk_rows(scores, k)
            sv.append(v)
            si.append(i)
        widths = [k // (a + 1) for a in range(k)]
        pad = -sum(widths) % SUB
        cand = jnp.concatenate([sv[0][a:a + 1, :] + sv[1][:w, :] for a, w in enumerate(widths)]
                               + [jnp.full((pad, sv[0].shape[1]), -jnp.inf, F32)], axis=0)
        pay = jnp.concatenate([si[0][a:a + 1, :] * PEER_NKEYS + si[1][:w, :] for a, w in enumerate(widths)]
                              + [jnp.zeros((pad, sv[0].shape[1]), jnp.int32)], axis=0)
        cv, ce = _topk_rows(cand, k, payload=pay)
        ex = jnp.exp(cv - cv[0:1, :])
        gates_ref[hd * k:(hd + 1) * k, :] = ex / jnp.sum(ex, axis=0, keepdims=True)
        eidx_ref[hd * k:(hd + 1) * k, :] = ce


def peer_route(h, g, w_q, sub_keys, *, tm=256):
    T, D = h.shape
    H, _, nkeys, half = sub_keys.shape
    npick = H * PEER_TOPK
    return pl.pallas_call(
        _peer_route_body,
        grid=(T // tm,),
        in_specs=[pl.BlockSpec((tm, D), lambda i: (i, 0)), _resident((1, D)),
                  _resident((H * 2 * half, D)), _resident((H * 2, nkeys, half))],
        out_specs=[pl.BlockSpec((npick, tm), lambda i: (0, i)),
                   pl.BlockSpec((npick, tm), lambda i: (0, i))],
        out_shape=[jax.ShapeDtypeStruct((npick, T), jnp.int32),
                   jax.ShapeDtypeStruct((npick, T), F32)],
        compiler_params=_params("parallel"),
        name="peer_route",
    )(h, g.reshape(1, D), w_q.T.astype(BF16), sub_keys.reshape(H * 2, nkeys, half).astype(BF16))


NPICK = PEER_HEADS * PEER_TOPK
NBUF = 3


def _expert_token(t, words_of_chunk, vs, h_ref, g_ref, gates_ref, o_ref, tok_lane):
    x = h_ref[pl.ds(t, 1), :]
    xn = _rms(x, g_ref[...])
    r = jnp.zeros((NPICK, LANE), F32)
    for c in range(SUB):
        words = words_of_chunk(c)
        u = pltpu.bitcast(words << 16, F32)
        vs[c] = pltpu.bitcast(words & HIGH_HALF, F32)
        r = r + u * xn[:, c * LANE:(c + 1) * LANE]
    dots = jnp.sum(r, axis=1, keepdims=True)
    gcol = jnp.sum(jnp.where(tok_lane == t, gates_ref[...], 0.0), axis=1, keepdims=True)
    coef = gcol * jax.nn.gelu(dots)
    ffn = jnp.concatenate(
        [jnp.sum(coef * vs[c], axis=0, keepdims=True) for c in range(SUB)], axis=1)
    o_ref[pl.ds(t, 1), :] = x + ffn


def _peer_expert_body(*refs, tt, nsteps, unroll, staged):
    if staged:
        (idx0, idx1, idx2, h_ref, g_ref, gates_ref, tab_ref, h2_ref, gates2_ref, rows_ref, o_ref, o2_ref,
         *scratch) = refs
    else:
        idx0, idx1, idx2, h_ref, g_ref, gates_ref, tab_ref, o_ref, *scratch = refs
    bufs, vscr, sem = scratch[:NBUF], scratch[NBUF], scratch[NBUF + 1]
    i = pl.program_id(0)
    tok_lane = lax.broadcasted_iota(jnp.int32, (NPICK, tt), 1)
    tok_lane2 = lax.broadcasted_iota(jnp.int32, (NPICK, tt // unroll), 1)

    def issue_token(idx_ref, t, p):
        for k in range(NPICK):
            row = pl.multiple_of((t * NPICK + k) * SUB, SUB)
            pltpu.make_async_copy(tab_ref.at[idx_ref[t, k]], bufs[p].at[pl.ds(row, SUB), :],
                                  sem.at[p]).start(priority=k % 2)

    def issue_step(idx_ref, p):
        def per_token(t, c):
            issue_token(idx_ref, t, p)
            return c
        lax.fori_loop(0, tt, per_token, 0)

    def compute_token(t, src, vs):
        base = pl.multiple_of(t * (NPICK * SUB), SUB)
        _expert_token(t, lambda c: src[pl.ds(base + c, NPICK, stride=SUB), :], vs,
                      h_ref, g_ref, gates_ref, o_ref, tok_lane)

    def compute_staged(t, vs):
        base = pl.multiple_of(t * NPICK, NPICK)
        _expert_token(t, lambda c: rows_ref[pl.ds(base, NPICK), c * LANE:(c + 1) * LANE], vs,
                      h2_ref, g_ref, gates2_ref, o2_ref, tok_lane2)

    @pl.when(i == 0)
    def _():
        issue_step(idx0, 0)
        if nsteps > 1:
            issue_step(idx1, 1)

    def step(p):
        src, ahead = bufs[p], (p + 2) % NBUF
        pltpu.make_async_copy(bufs[ahead], src, sem.at[p]).wait()

        def tokens(with_issue):
            def body(j, c):
                for r in range(unroll):
                    t = j * unroll + r
                    if with_issue:
                        issue_token(idx2, t, ahead)
                    compute_token(t, src, vscr.at[r])
                if staged:
                    compute_staged(j, vscr.at[unroll])
                return c
            lax.fori_loop(0, tt // unroll, body, 0)

        @pl.when(i + 2 < nsteps)
        def _():
            tokens(True)

        @pl.when(i + 2 >= nsteps)
        def _():
            tokens(False)

    for p in range(NBUF):
        @pl.when(i % NBUF == p)
        def _(p=p):
            step(p)


def pack_uv(u, v):
    E, D = u.shape
    ub = lax.bitcast_convert_type(u.astype(BF16), jnp.uint16).astype(jnp.uint32)
    vb = lax.bitcast_convert_type(v.astype(BF16), jnp.uint16).astype(jnp.uint32)
    return lax.bitcast_convert_type(ub | (vb << 16), jnp.int32).reshape(E, SUB, D // SUB)


def peer_expert(h, g, eidx, gates_t, tab, staged=None, *, tt=16, unroll=2):
    T, D = h.shape
    nsteps = T // tt
    ts = tt // unroll
    tile = pl.BlockSpec((tt, D), lambda i: (i, 0))
    in_specs = [
        pl.BlockSpec((tt, NPICK), lambda i: (i, 0), memory_space=pltpu.SMEM),
        pl.BlockSpec((tt, NPICK), lambda i: (jnp.minimum(i + 1, nsteps - 1), 0), memory_space=pltpu.SMEM),
        pl.BlockSpec((tt, NPICK), lambda i: (jnp.minimum(i + 2, nsteps - 1), 0), memory_space=pltpu.SMEM),
        tile, _resident((1, D)), pl.BlockSpec((None, NPICK, tt), lambda i: (i, 0, 0)),
        pl.BlockSpec(memory_space=pl.ANY)]
    args = [eidx, eidx, eidx, h, g.reshape(1, D), gates_t, tab]
    out_specs, out_shape = tile, jax.ShapeDtypeStruct((T, D), F32)
    if staged is not None:
        tile2 = pl.BlockSpec((ts, D), lambda i: (i, 0))
        in_specs += [tile2, pl.BlockSpec((None, NPICK, ts), lambda i: (i, 0, 0)),
                     pl.BlockSpec((ts * NPICK, D), lambda i: (i, 0))]
        args += list(staged)
        out_specs, out_shape = [tile, tile2], [out_shape, jax.ShapeDtypeStruct((T // unroll, D), F32)]
    return pl.pallas_call(
        functools.partial(_peer_expert_body, tt=tt, nsteps=nsteps, unroll=unroll, staged=staged is not None),
        grid=(nsteps,),
        in_specs=in_specs,
        out_specs=out_specs,
        out_shape=out_shape,
        scratch_shapes=[
            *[pltpu.VMEM((tt * NPICK * SUB, LANE), jnp.int32)] * NBUF,
            pltpu.VMEM((unroll + 1, SUB, NPICK, LANE), F32),
            pltpu.SemaphoreType.DMA((NBUF,)),
        ],
        compiler_params=_params("arbitrary"),
        name="peer_expert_mixed" if staged is not None else "peer_expert",
    )(*args)


SC_ROWS = 32
SC_IDX = 128


def sc_gather_rows(tab, idx):
    _, W = tab.shape
    N = idx.shape[0]
    info = plsc.get_sparse_core_info()
    nw = info.num_cores * info.num_subcores
    per_w = N // nw
    assert per_w * nw == N and per_w % SC_IDX == 0
    mesh = plsc.VectorSubcoreMesh(core_axis_name="core", subcore_axis_name="subcore")

    @functools.partial(
        pl.kernel, out_type=jax.ShapeDtypeStruct((N, W), tab.dtype), mesh=mesh,
        scratch_types=[pltpu.VMEM((SC_IDX,), jnp.int32), pltpu.VMEM((SC_ROWS, W), tab.dtype),
                       pltpu.VMEM((SC_ROWS, W), tab.dtype), pltpu.SemaphoreType.DMA, pltpu.SemaphoreType.DMA,
                       pltpu.SemaphoreType.DMA])
    def gather(tab_hbm, idx_hbm, out_hbm, idx_v, rows0, rows1, gsem, wsem0, wsem1):
        wid = lax.axis_index("subcore") * info.num_cores + lax.axis_index("core")
        base = wid * per_w
        bufs = ((rows0, wsem0), (rows1, wsem1))

        @pl.loop(0, per_w // SC_IDX)
        def _(c):
            off = base + c * SC_IDX
            pltpu.sync_copy(idx_hbm.at[pl.ds(off, SC_IDX)], idx_v)
            for j in range(SC_IDX // SC_ROWS):
                buf, wsem = bufs[j % 2]
                dst = out_hbm.at[pl.ds(off + j * SC_ROWS, SC_ROWS)]

                @pl.when((c > 0) | (j >= 2))
                def _():
                    pltpu.make_async_copy(buf, dst, wsem).wait()

                pltpu.async_copy(tab_hbm.at[idx_v.at[pl.ds(j * SC_ROWS, SC_ROWS)]], buf, gsem).wait()
                pltpu.async_copy(buf, dst, wsem)

        for buf, wsem in bufs:
            pltpu.make_async_copy(buf, out_hbm.at[pl.ds(base, SC_ROWS)], wsem).wait()

    return gather(tab, idx)


PEER_LEAD = 2048
PEER_CALLS = 5


def _tile_gates(gates_t, tt):
    npick, T = gates_t.shape
    return gates_t.reshape(npick, T // tt, tt).transpose(1, 0, 2)


def peer_ffn(h, g, w_q, sub_keys, u_tab, v_tab, *, tt=16, unroll=2):
    T, D = h.shape
    eidx_t, gates_t = peer_route(h, g, w_q, sub_keys)
    eidx = eidx_t.T
    tab = pack_uv(u_tab, v_tab)
    tab2d = tab.reshape(tab.shape[0], D)
    part = (T - PEER_LEAD) // ((unroll + 1) * PEER_CALLS)
    assert PEER_LEAD + (unroll + 1) * PEER_CALLS * part == T and part % 32 == 0 and PEER_LEAD % tt == 0
    gates = lambda lo, hi, tile: _tile_gates(gates_t[:, lo:hi], tile)
    outs = [peer_expert(h[:PEER_LEAD], g, eidx[:PEER_LEAD], gates(0, PEER_LEAD, tt), tab, tt=tt, unroll=unroll)]
    for c in range(PEER_CALLS):
        lo = PEER_LEAD + (unroll + 1) * c * part
        mid, hi = lo + unroll * part, lo + (unroll + 1) * part
        rows = sc_gather_rows(tab2d, eidx[mid:hi].reshape(-1))
        outs += peer_expert(h[lo:mid], g, eidx[lo:mid], gates(lo, mid, tt), tab,
                            staged=(h[mid:hi], gates(mid, hi, tt // unroll), rows), tt=tt, unroll=unroll)
    return jnp.concatenate(outs, axis=0)


def _ple_body(h_ref, p_ref, g_ref, wg_ref, wp_ref, gf_ref, o_ref, *, final):
    x = h_ref[...]
    gate = jax.nn.sigmoid(jnp.dot(_rms(x, g_ref[...]).astype(BF16), wg_ref[...], preferred_element_type=F32))
    emb = jnp.dot(p_ref[...].astype(BF16), wp_ref[...], preferred_element_type=F32)
    y = x + gate * emb
    o_ref[...] = _rms(y, gf_ref[...]) if final else y


def per_layer_embed(h, p, g, w_gate, w_proj, g_final, *, final, tm=512):
    T, D = h.shape
    P = p.shape[1]
    tile = pl.BlockSpec((tm, D), lambda i: (i, 0))
    return pl.pallas_call(
        functools.partial(_ple_body, final=final),
        grid=(T // tm,),
        in_specs=[tile, pl.BlockSpec((tm, P), lambda i: (i, 0)), _resident((1, D)), _resident((D, D)),
                  _resident((P, D)), _resident((1, D))],
        out_specs=tile,
        out_shape=jax.ShapeDtypeStruct((T, D), F32),
        compiler_params=_params("parallel"),
        name="per_layer_embed",
    )(h, p, g.reshape(1, D), w_gate.astype(BF16), w_proj.astype(BF16), g_final.reshape(1, D))


def kernel(x, p, norm_mix, a_w_in, a_conv_w, a_conv_b, a_w_r, a_w_i, a_b_r, a_b_i, a_lambda,
           a_w_out, kv_norm, w_kv, b_w_q, b_w_o, norm_ffn, peer_w_q, peer_sub_keys, peer_u,
           peer_v, norm_ple, ple_w_gate, ple_w_proj, final_norm):
    B, S, D = x.shape
    T = B * S
    depth = norm_mix.shape[0]
    n_a = a_w_in.shape[0]
    h = x.reshape(T, D)
    k_sh = v_sh = None
    for i in range(depth):
        if i < n_a:
            j = i
            h = rglru_mixer(h, norm_mix[i], a_w_in[j], a_conv_w[j], a_conv_b[j], a_w_r[j], a_w_i[j],
                            a_b_r[j], a_b_i[j], a_lambda[j], a_w_out[j], batch=B)
        else:
            j = i - n_a
            q, k_new, v_new = qkv_project(h, norm_mix[i], kv_norm, b_w_q[j], w_kv)
            if i == n_a:
                k_sh, v_sh = k_new, v_new
            o = stick_breaking_attention(q, k_sh, v_sh, batch=B)
            h = matmul_residual(h, o, b_w_o[j])
        h = peer_ffn(h, norm_ffn[i], peer_w_q[i], peer_sub_keys[i], peer_u[i], peer_v[i])
        h = per_layer_embed(h, p[i].reshape(T, -1), norm_ple[i], ple_w_gate[i], ple_w_proj[i], final_norm,
                            final=(i == depth - 1))
    return h.reshape(B, S, D)
```

```python
import functools

import jax
import jax.numpy as jnp
from jax import lax
from jax.experimental import pallas as pl
from jax.experimental.pallas import tpu as pltpu
from jax.experimental.pallas import tpu_sc as plsc

F32 = jnp.float32
BF16 = jnp.bfloat16

EPS = 1e-6
LRU_HEADS = 8
CONV_WIDTH = 4
LRU_C = 8.0
ATTN_HEADS = 16
ATTN_HEAD_DIM = 64
PEER_HEADS = 8
PEER_NKEYS = 128
PEER_TOPK = 16

SUB, LANE = 8, 128
HIGH_HALF = -65536
VMEM_LIMIT = 48 * 1024 * 1024


def _params(*semantics):
    return pltpu.CompilerParams(dimension_semantics=semantics, vmem_limit_bytes=VMEM_LIMIT)


def _rms(x, g):
    return x * lax.rsqrt(jnp.mean(x * x, axis=-1, keepdims=True) + EPS) * g


def _softplus(z):
    return jnp.maximum(z, 0.0) + jnp.log1p(jnp.exp(-jnp.abs(z)))


def _resident(shape):
    return pl.BlockSpec(shape, lambda *_: (0,) * len(shape))


def _rglru_body(h_ref, g_ref, win_ref, cw_ref, cb_ref, wr_ref, wi_ref, br_ref, bi_ref, lam_ref,
                wout_ref, o_ref, tail_ref, state_ref):
    ts, _ = h_ref.shape
    width = wout_ref.shape[0]
    blk = width // LRU_HEADS

    @pl.when(pl.program_id(1) == 0)
    def _():
        tail_ref[...] = jnp.zeros_like(tail_ref)
        state_ref[...] = jnp.zeros_like(state_ref)

    x = h_ref[...]
    proj = jnp.dot(_rms(x, g_ref[...]).astype(BF16), win_ref[...], preferred_element_type=F32)
    yb, xb = proj[:, :width], proj[:, width:]

    row = lax.broadcasted_iota(jnp.int32, (ts, width), 0)
    ext = jnp.concatenate([tail_ref[...], xb], axis=0)
    xc = cw_ref[CONV_WIDTH - 1:CONV_WIDTH, :] * xb + cb_ref[...]
    for d in range(1, CONV_WIDTH):
        xc = xc + cw_ref[CONV_WIDTH - 1 - d:CONV_WIDTH - d, :] * pltpu.roll(ext, d, axis=0)[SUB:, :]
    tail_ref[...] = xb[ts - SUB:, :]

    xcb = xc.astype(BF16)
    r = jnp.concatenate([jnp.dot(xcb[:, k * blk:(k + 1) * blk], wr_ref[k], preferred_element_type=F32)
                         for k in range(LRU_HEADS)], axis=1)
    ig = jnp.concatenate([jnp.dot(xcb[:, k * blk:(k + 1) * blk], wi_ref[k], preferred_element_type=F32)
                          for k in range(LRU_HEADS)], axis=1)
    r = jax.nn.sigmoid(r + br_ref[...])
    ig = jax.nn.sigmoid(ig + bi_ref[...])
    log_a = -LRU_C * r * _softplus(-lam_ref[...])
    a = jnp.exp(log_a)
    u = jnp.sqrt(-jnp.tanh(log_a) * (a * a + 1.0)) * ig * xc

    d = 1
    while d < ts:
        keep = row >= d
        a_prev = jnp.where(keep, pltpu.roll(a, d, axis=0), 1.0)
        u_prev = jnp.where(keep, pltpu.roll(u, d, axis=0), 0.0)
        u = a * u_prev + u
        a = a * a_prev
        d *= 2
    hseq = u + a * state_ref[SUB - 1:SUB, :]
    state_ref[...] = hseq[ts - SUB:, :]

    gated = (jax.nn.gelu(yb) * hseq).astype(BF16)
    o_ref[...] = x + jnp.dot(gated, wout_ref[...], preferred_element_type=F32)


def rglru_mixer(h, g, w_in, conv_w, conv_b, w_r, w_i, b_r, b_i, lam, w_out, *, batch, ts=256):
    T, D = h.shape
    S = T // batch
    nt = S // ts
    W = w_out.shape[0]
    tile = pl.BlockSpec((ts, D), lambda b, j: (b * nt + j, 0))
    vec = lambda a: a.reshape(1, -1)
    return pl.pallas_call(
        _rglru_body,
        grid=(batch, nt),
        in_specs=[tile, _resident((1, D)), _resident((D, 2 * W)), _resident((CONV_WIDTH, W)),
                  _resident((1, W)), _resident(w_r.shape), _resident(w_i.shape), _resident((1, W)),
                  _resident((1, W)), _resident((1, W)), _resident((W, D))],
        out_specs=tile,
        out_shape=jax.ShapeDtypeStruct((T, D), F32),
        scratch_shapes=[pltpu.VMEM((SUB, W), F32), pltpu.VMEM((SUB, W), F32)],
        compiler_params=_params("parallel", "arbitrary"),
        name="rglru_mixer",
    )(h, vec(g), w_in.astype(BF16), conv_w, vec(conv_b), w_r.astype(BF16), w_i.astype(BF16),
      vec(b_r), vec(b_i), vec(lam), w_out.astype(BF16))


def _qkv_body(h_ref, gq_ref, gkv_ref, wq_ref, wkv_ref, q_ref, k_ref, v_ref, *, scale):
    x = h_ref[...]
    xhat = x * lax.rsqrt(jnp.mean(x * x, axis=-1, keepdims=True) + EPS)
    q = jnp.dot((xhat * gq_ref[...]).astype(BF16), wq_ref[...], preferred_element_type=F32)
    kv = jnp.dot((xhat * gkv_ref[...]).astype(BF16), wkv_ref[...], preferred_element_type=F32)
    aw = k_ref.shape[-1]
    q_ref[...] = (q * scale).astype(q_ref.dtype)
    k_ref[...] = kv[:, :aw].astype(k_ref.dtype)
    v_ref[...] = kv[:, aw:].astype(v_ref.dtype)


def qkv_project(h, g_q, g_kv, w_q, w_kv, *, tm=512):
    T, D = h.shape
    aw = w_q.shape[1]
    tile = pl.BlockSpec((tm, D), lambda i: (i, 0))
    out = pl.BlockSpec((tm, aw), lambda i: (i, 0))
    return pl.pallas_call(
        functools.partial(_qkv_body, scale=ATTN_HEAD_DIM ** -0.5),
        grid=(T // tm,),
        in_specs=[tile, _resident((1, D)), _resident((1, D)), _resident((D, aw)), _resident((D, 2 * aw))],
        out_specs=[out, out, out],
        out_shape=[jax.ShapeDtypeStruct((T, aw), BF16)] * 3,
        compiler_params=_params("parallel"),
        name="qkv_project",
    )(h, g_q.reshape(1, D), g_kv.reshape(1, D), w_q.astype(BF16), w_kv.astype(BF16))


ATTN_TQ, ATTN_TK = 512, 256


def _stick_body(q_ref, k_ref, v_ref, o_ref, *, tq, tk, dh):
    S, lanes = q_ref.shape
    nh = lanes // dh
    ndiag = tq // tk
    row = lax.broadcasted_iota(jnp.int32, (tk, tk), 0)
    col = lax.broadcasted_iota(jnp.int32, (tk, tk), 1)
    neg_from = jnp.where(row >= col, -1.0, 0.0).astype(BF16)
    qpos = lax.broadcasted_iota(jnp.int32, (tq, tk), 0)
    kpos = lax.broadcasted_iota(jnp.int32, (tq, tk), 1)
    head_of_lane = lax.broadcasted_iota(jnp.int32, (tq, lanes), 1) // dh

    def pair(q, k, v, carry, shift):
        acc, right = carry
        z = lax.dot_general(q, k, (((1,), (1,)), ((), ())), preferred_element_type=F32)
        sp = jnp.maximum(z, 0.0) + jnp.log(1.0 + jnp.exp(-jnp.abs(z)))
        if shift is not None:
            causal = kpos < qpos + shift
            sp = jnp.where(causal, sp, 0.0)
        hi = pltpu.bitcast(pltpu.bitcast(sp, jnp.int32) & HIGH_HALF, F32)
        lo = sp - hi
        tail = (jnp.dot(hi.astype(BF16), neg_from, preferred_element_type=F32)
                + jnp.dot(lo.astype(BF16), neg_from, preferred_element_type=F32))
        a = jnp.exp(z + tail + right)
        if shift is not None:
            a = jnp.where(causal, a, 0.0)
        acc = acc + jnp.dot(a.astype(BF16), v, preferred_element_type=F32)
        return acc, right + tail[:, 0:1]

    def pairs(qs, kb, carries, shift):
        k = k_ref[pl.ds(pl.multiple_of(kb * tk, tk), tk), :]
        v = v_ref[pl.ds(pl.multiple_of(kb * tk, tk), tk), :]
        return tuple(pair(qs[h], k, v, carries[h], shift) for h in range(nh))

    def q_block(qi, c):
        q = q_ref[pl.ds(pl.multiple_of(qi * tq, tq), tq), :]
        qs = [jnp.where(head_of_lane == h, q, jnp.zeros_like(q)) for h in range(nh)]
        carries = tuple((jnp.zeros((tq, lanes), F32), jnp.zeros((tq, 1), F32)) for _ in range(nh))
        last = (qi + 1) * ndiag - 1
        for d in range(ndiag):
            carries = pairs(qs, last - d, carries, (d + 1 - ndiag) * tk)

        def past(n, cr):
            for d in range(ndiag):
                cr = pairs(qs, (qi - n) * ndiag - 1 - d, cr, None)
            return cr

        carries = lax.fori_loop(0, qi, past, carries)
        out = carries[0][0]
        for h in range(1, nh):
            out = jnp.where(head_of_lane == h, carries[h][0], out)
        o_ref[pl.ds(pl.multiple_of(qi * tq, tq), tq), :] = out.astype(o_ref.dtype)
        return c

    lax.fori_loop(0, S // tq, q_block, 0)


def stick_breaking_attention(q, k, v, *, batch, tq=ATTN_TQ, tk=ATTN_TK):
    T, width = q.shape
    S = T // batch
    tq = min(tq, S)
    tk = min(tk, tq)
    spec = pl.BlockSpec((S, LANE), lambda b, g: (b, g))
    return pl.pallas_call(
        functools.partial(_stick_body, tq=tq, tk=tk, dh=ATTN_HEAD_DIM),
        grid=(batch, width // LANE),
        in_specs=[spec, spec, spec],
        out_specs=spec,
        out_shape=jax.ShapeDtypeStruct((T, width), BF16),
        compiler_params=_params("parallel", "parallel"),
        name="stick_breaking",
    )(q, k, v)


def _matmul_residual_body(h_ref, x_ref, w_ref, o_ref):
    o_ref[...] = h_ref[...] + jnp.dot(x_ref[...], w_ref[...], preferred_element_type=F32)


def matmul_residual(h, x, w, *, tm=512):
    T, D = h.shape
    K = x.shape[1]
    return pl.pallas_call(
        _matmul_residual_body,
        grid=(T // tm,),
        in_specs=[pl.BlockSpec((tm, D), lambda i: (i, 0)), pl.BlockSpec((tm, K), lambda i: (i, 0)),
                  _resident((K, D))],
        out_specs=pl.BlockSpec((tm, D), lambda i: (i, 0)),
        out_shape=jax.ShapeDtypeStruct((T, D), F32),
        compiler_params=_params("parallel"),
        name="matmul_residual",
    )(h, x, w.astype(BF16))


def _topk_rows(s, k, payload=None):
    n = s.shape[0]
    rows = lax.broadcasted_iota(jnp.int32, s.shape, 0)
    vals, sel = [], []
    for _ in range(k):
        best = jnp.max(s, axis=0, keepdims=True)
        pos = jnp.min(jnp.where(s == best, rows, n), axis=0, keepdims=True)
        hit = rows == pos
        vals.append(best)
        sel.append(pos if payload is None else jnp.max(jnp.where(hit, payload, -1), axis=0, keepdims=True))
        s = jnp.where(hit, -jnp.inf, s)
    return jnp.concatenate(vals, axis=0), jnp.concatenate(sel, axis=0)


def _peer_route_body(h_ref, g_ref, wqt_ref, keys_ref, eidx_ref, gates_ref):
    xn = _rms(h_ref[...], g_ref[...]).astype(BF16)
    qt = lax.dot_general(wqt_ref[...], xn, (((1,), (1,)), ((), ())), preferred_element_type=F32)
    half = keys_ref.shape[-1]
    k = PEER_TOPK
    for hd in range(PEER_HEADS):
        sv, si = [], []
        for part in range(2):
            hp = hd * 2 + part
            q = qt[hp * half:(hp + 1) * half, :].astype(BF16)
            scores = jnp.dot(keys_ref[hp], q, preferred_element_type=F32)
            v, i = _topk_rows(scores, k)
            sv.append(v)
            si.append(i)
        widths = [k // (a + 1) for a in range(k)]
        pad = -sum(widths) % SUB
        cand = jnp.concatenate([sv[0][a:a + 1, :] + sv[1][:w, :] for a, w in enumerate(widths)]
                               + [jnp.full((pad, sv[0].shape[1]), -jnp.inf, F32)], axis=0)
        pay = jnp.concatenate([si[0][a:a + 1, :] * PEER_NKEYS + si[1][:w, :] for a, w in enumerate(widths)]
                              + [jnp.zeros((pad, sv[0].shape[1]), jnp.int32)], axis=0)
        cv, ce = _topk_rows(cand, k, payload=pay)
        ex = jnp.exp(cv - cv[0:1, :])
        gates_ref[hd * k:(hd + 1) * k, :] = ex / jnp.sum(ex, axis=0, keepdims=True)
        eidx_ref[hd * k:(hd + 1) * k, :] = ce


def peer_route(h, g, w_q, sub_keys, *, tm=256):
    T, D = h.shape
    H, _, nkeys, half = sub_keys.shape
    npick = H * PEER_TOPK
    return pl.pallas_call(
        _peer_route_body,
        grid=(T // tm,),
        in_specs=[pl.BlockSpec((tm, D), lambda i: (i, 0)), _resident((1, D)),
                  _resident((H * 2 * half, D)), _resident((H * 2, nkeys, half))],
        out_specs=[pl.BlockSpec((npick, tm), lambda i: (0, i)),
                   pl.BlockSpec((npick, tm), lambda i: (0, i))],
        out_shape=[jax.ShapeDtypeStruct((npick, T), jnp.int32),
                   jax.ShapeDtypeStruct((npick, T), F32)],
        compiler_params=_params("parallel"),
        name="peer_route",
    )(h, g.reshape(1, D), w_q.T.astype(BF16), sub_keys.reshape(H * 2, nkeys, half).astype(BF16))


NPICK = PEER_HEADS * PEER_TOPK
NBUF = 3


def _expert_token(t, words_of_chunk, vs, h_ref, g_ref, gates_ref, o_ref, tok_lane):
    x = h_ref[pl.ds(t, 1), :]
    xn = _rms(x, g_ref[...])
    r = jnp.zeros((NPICK, LANE), F32)
    for c in range(SUB):
        words = words_of_chunk(c)
        u = pltpu.bitcast(words << 16, F32)
        vs[c] = pltpu.bitcast(words & HIGH_HALF, F32)
        r = r + u * xn[:, c * LANE:(c + 1) * LANE]
    dots = jnp.sum(r, axis=1, keepdims=True)
    gcol = jnp.sum(jnp.where(tok_lane == t, gates_ref[...], 0.0), axis=1, keepdims=True)
    coef = gcol * jax.nn.gelu(dots)
    ffn = jnp.concatenate(
        [jnp.sum(coef * vs[c], axis=0, keepdims=True) for c in range(SUB)], axis=1)
    o_ref[pl.ds(t, 1), :] = x + ffn


def _peer_expert_body(*refs, tt, nsteps, unroll, staged):
    if staged:
        (idx0, idx1, idx2, h_ref, g_ref, gates_ref, tab_ref, h2_ref, gates2_ref, rows_ref, o_ref, o2_ref,
         *scratch) = refs
    else:
        idx0, idx1, idx2, h_ref, g_ref, gates_ref, tab_ref, o_ref, *scratch = refs
    bufs, vscr, sem = scratch[:NBUF], scratch[NBUF], scratch[NBUF + 1]
    i = pl.program_id(0)
    tok_lane = lax.broadcasted_iota(jnp.int32, (NPICK, tt), 1)
    tok_lane2 = lax.broadcasted_iota(jnp.int32, (NPICK, tt // unroll), 1)

    def issue_token(idx_ref, t, p):
        for k in range(NPICK):
            row = pl.multiple_of((t * NPICK + k) * SUB, SUB)
            pltpu.make_async_copy(tab_ref.at[idx_ref[t, k]], bufs[p].at[pl.ds(row, SUB), :],
                                  sem.at[p]).start(priority=k % 2)

    def issue_step(idx_ref, p):
        def per_token(t, c):
            issue_token(idx_ref, t, p)
            return c
        lax.fori_loop(0, tt, per_token, 0)

    def compute_token(t, src, vs):
        base = pl.multiple_of(t * (NPICK * SUB), SUB)
        _expert_token(t, lambda c: src[pl.ds(base + c, NPICK, stride=SUB), :], vs,
                      h_ref, g_ref, gates_ref, o_ref, tok_lane)

    def compute_staged(t, vs):
        base = pl.multiple_of(t * NPICK, NPICK)
        _expert_token(t, lambda c: rows_ref[pl.ds(base, NPICK), c * LANE:(c + 1) * LANE], vs,
                      h2_ref, g_ref, gates2_ref, o2_ref, tok_lane2)

    @pl.when(i == 0)
    def _():
        issue_step(idx0, 0)
        if nsteps > 1:
            issue_step(idx1, 1)

    def step(p):
        src, ahead = bufs[p], (p + 2) % NBUF
        pltpu.make_async_copy(bufs[ahead], src, sem.at[p]).wait()

        def tokens(with_issue):
            def body(j, c):
                for r in range(unroll):
                    t = j * unroll + r
                    if with_issue:
                        issue_token(idx2, t, ahead)
                    compute_token(t, src, vscr.at[r])
                if staged:
                    compute_staged(j, vscr.at[unroll])
                return c
            lax.fori_loop(0, tt // unroll, body, 0)

        @pl.when(i + 2 < nsteps)
        def _():
            tokens(True)

        @pl.when(i + 2 >= nsteps)
        def _():
            tokens(False)

    for p in range(NBUF):
        @pl.when(i % NBUF == p)
        def _(p=p):
            step(p)


def pack_uv(u, v):
    E, D = u.shape
    ub = lax.bitcast_convert_type(u.astype(BF16), jnp.uint16).astype(jnp.uint32)
    vb = lax.bitcast_convert_type(v.astype(BF16), jnp.uint16).astype(jnp.uint32)
    return lax.bitcast_convert_type(ub | (vb << 16), jnp.int32).reshape(E, SUB, D // SUB)


def peer_expert(h, g, eidx, gates_t, tab, staged=None, *, tt=16, unroll=2):
    T, D = h.shape
    nsteps = T // tt
    ts = tt // unroll
    tile = pl.BlockSpec((tt, D), lambda i: (i, 0))
    in_specs = [
        pl.BlockSpec((tt, NPICK), lambda i: (i, 0), memory_space=pltpu.SMEM),
        pl.BlockSpec((tt, NPICK), lambda i: (jnp.minimum(i + 1, nsteps - 1), 0), memory_space=pltpu.SMEM),
        pl.BlockSpec((tt, NPICK), lambda i: (jnp.minimum(i + 2, nsteps - 1), 0), memory_space=pltpu.SMEM),
        tile, _resident((1, D)), pl.BlockSpec((None, NPICK, tt), lambda i: (i, 0, 0)),
        pl.BlockSpec(memory_space=pl.ANY)]
    args = [eidx, eidx, eidx, h, g.reshape(1, D), gates_t, tab]
    out_specs, out_shape = tile, jax.ShapeDtypeStruct((T, D), F32)
    if staged is not None:
        tile2 = pl.BlockSpec((ts, D), lambda i: (i, 0))
        in_specs += [tile2, pl.BlockSpec((None, NPICK, ts), lambda i: (i, 0, 0)),
                     pl.BlockSpec((ts * NPICK, D), lambda i: (i, 0))]
        args += list(staged)
        out_specs, out_shape = [tile, tile2], [out_shape, jax.ShapeDtypeStruct((T // unroll, D), F32)]
    return pl.pallas_call(
        functools.partial(_peer_expert_body, tt=tt, nsteps=nsteps, unroll=unroll, staged=staged is not None),
        grid=(nsteps,),
        in_specs=in_specs,
        out_specs=out_specs,
        out_shape=out_shape,
        scratch_shapes=[
            *[pltpu.VMEM((tt * NPICK * SUB, LANE), jnp.int32)] * NBUF,
            pltpu.VMEM((unroll + 1, SUB, NPICK, LANE), F32),
            pltpu.SemaphoreType.DMA((NBUF,)),
        ],
        compiler_params=_params("arbitrary"),
        name="peer_expert_mixed" if staged is not None else "peer_expert",
    )(*args)


SC_ROWS = 32
SC_IDX = 128
SC_WORKERS = 24


def sc_gather_rows(tab, idx):
    _, W = tab.shape
    N = idx.shape[0]
    info = plsc.get_sparse_core_info()
    assert SC_WORKERS <= info.num_cores * info.num_subcores
    per_w = N // SC_WORKERS
    assert per_w * SC_WORKERS == N and per_w % SC_IDX == 0
    mesh = plsc.VectorSubcoreMesh(core_axis_name="core", subcore_axis_name="subcore")

    @functools.partial(
        pl.kernel, out_type=jax.ShapeDtypeStruct((N, W), tab.dtype), mesh=mesh,
        scratch_types=[pltpu.VMEM((SC_IDX,), jnp.int32), pltpu.VMEM((SC_ROWS, W), tab.dtype),
                       pltpu.VMEM((SC_ROWS, W), tab.dtype), pltpu.SemaphoreType.DMA, pltpu.SemaphoreType.DMA,
                       pltpu.SemaphoreType.DMA])
    def gather(tab_hbm, idx_hbm, out_hbm, idx_v, rows0, rows1, gsem, wsem0, wsem1):
        wid = lax.axis_index("subcore") * info.num_cores + lax.axis_index("core")
        base = wid * per_w
        bufs = ((rows0, wsem0), (rows1, wsem1))

        @pl.when(wid < SC_WORKERS)
        def _():
            @pl.loop(0, per_w // SC_IDX)
            def _(c):
                off = base + c * SC_IDX
                pltpu.sync_copy(idx_hbm.at[pl.ds(off, SC_IDX)], idx_v)
                for j in range(SC_IDX // SC_ROWS):
                    buf, wsem = bufs[j % 2]
                    dst = out_hbm.at[pl.ds(off + j * SC_ROWS, SC_ROWS)]

                    @pl.when((c > 0) | (j >= 2))
                    def _():
                        pltpu.make_async_copy(buf, dst, wsem).wait()

                    pltpu.async_copy(tab_hbm.at[idx_v.at[pl.ds(j * SC_ROWS, SC_ROWS)]], buf, gsem).wait()
                    pltpu.async_copy(buf, dst, wsem)

            for buf, wsem in bufs:
                pltpu.make_async_copy(buf, out_hbm.at[pl.ds(base, SC_ROWS)], wsem).wait()

    return gather(tab, idx)


PEER_LEAD = 2528
PEER_CALLS = 5


def _tile_gates(gates_t, tt):
    npick, T = gates_t.shape
    return gates_t.reshape(npick, T // tt, tt).transpose(1, 0, 2)


def peer_ffn(h, g, w_q, sub_keys, u_tab, v_tab, *, tt=16, unroll=2):
    T, D = h.shape
    eidx_t, gates_t = peer_route(h, g, w_q, sub_keys)
    eidx = eidx_t.T
    tab = pack_uv(u_tab, v_tab)
    tab2d = tab.reshape(tab.shape[0], D)
    part = (T - PEER_LEAD) // ((unroll + 1) * PEER_CALLS)
    assert PEER_LEAD + (unroll + 1) * PEER_CALLS * part == T and part % tt == 0 and PEER_LEAD % tt == 0
    gates = lambda lo, hi, tile: _tile_gates(gates_t[:, lo:hi], tile)
    outs = [peer_expert(h[:PEER_LEAD], g, eidx[:PEER_LEAD], gates(0, PEER_LEAD, tt), tab, tt=tt, unroll=unroll)]
    for c in range(PEER_CALLS):
        lo = PEER_LEAD + (unroll + 1) * c * part
        mid, hi = lo + unroll * part, lo + (unroll + 1) * part
        rows = sc_gather_rows(tab2d, eidx[mid:hi].reshape(-1))
        outs += peer_expert(h[lo:mid], g, eidx[lo:mid], gates(lo, mid, tt), tab,
                            staged=(h[mid:hi], gates(mid, hi, tt // unroll), rows), tt=tt, unroll=unroll)
    return jnp.concatenate(outs, axis=0)


def _ple_body(h_ref, p_ref, g_ref, wg_ref, wp_ref, gf_ref, o_ref, *, final):
    x = h_ref[...]
    gate = jax.nn.sigmoid(jnp.dot(_rms(x, g_ref[...]).astype(BF16), wg_ref[...], preferred_element_type=F32))
    emb = jnp.dot(p_ref[...].astype(BF16), wp_ref[...], preferred_element_type=F32)
    y = x + gate * emb
    o_ref[...] = _rms(y, gf_ref[...]) if final else y


def per_layer_embed(h, p, g, w_gate, w_proj, g_final, *, final, tm=512):
    T, D = h.shape
    P = p.shape[1]
    tile = pl.BlockSpec((tm, D), lambda i: (i, 0))
    return pl.pallas_call(
        functools.partial(_ple_body, final=final),
        grid=(T // tm,),
        in_specs=[tile, pl.BlockSpec((tm, P), lambda i: (i, 0)), _resident((1, D)), _resident((D, D)),
                  _resident((P, D)), _resident((1, D))],
        out_specs=tile,
        out_shape=jax.ShapeDtypeStruct((T, D), F32),
        compiler_params=_params("parallel"),
        name="per_layer_embed",
    )(h, p, g.reshape(1, D), w_gate.astype(BF16), w_proj.astype(BF16), g_final.reshape(1, D))


def kernel(x, p, norm_mix, a_w_in, a_conv_w, a_conv_b, a_w_r, a_w_i, a_b_r, a_b_i, a_lambda,
           a_w_out, kv_norm, w_kv, b_w_q, b_w_o, norm_ffn, peer_w_q, peer_sub_keys, peer_u,
           peer_v, norm_ple, ple_w_gate, ple_w_proj, final_norm):
    B, S, D = x.shape
    T = B * S
    depth = norm_mix.shape[0]
    n_a = a_w_in.shape[0]
    h = x.reshape(T, D)
    k_sh = v_sh = None
    for i in range(depth):
        if i < n_a:
            j = i
            h = rglru_mixer(h, norm_mix[i], a_w_in[j], a_conv_w[j], a_conv_b[j], a_w_r[j], a_w_i[j],
                            a_b_r[j], a_b_i[j], a_lambda[j], a_w_out[j], batch=B)
        else:
            j = i - n_a
            q, k_new, v_new = qkv_project(h, norm_mix[i], kv_norm, b_w_q[j], w_kv)
            if i == n_a:
                k_sh, v_sh = k_new, v_new
            o = stick_breaking_attention(q, k_sh, v_sh, batch=B)
            h = matmul_residual(h, o, b_w_o[j])
        h = peer_ffn(h, norm_ffn[i], peer_w_q[i], peer_sub_keys[i], peer_u[i], peer_v[i])
        h = per_layer_embed(h, p[i].reshape(T, -1), norm_ple[i], ple_w_gate[i], ple_w_proj[i], final_norm,
                            final=(i == depth - 1))
    return h.reshape(B, S, D)
```

```python
import functools

import jax
import jax.numpy as jnp
from jax import lax
from jax.experimental import pallas as pl
from jax.experimental.pallas import tpu as pltpu
from jax.experimental.pallas import tpu_sc as plsc

F32 = jnp.float32
BF16 = jnp.bfloat16

EPS = 1e-6
LRU_HEADS = 8
CONV_WIDTH = 4
LRU_C = 8.0
ATTN_HEADS = 16
ATTN_HEAD_DIM = 64
PEER_HEADS = 8
PEER_NKEYS = 128
PEER_TOPK = 16

SUB, LANE = 8, 128
HIGH_HALF = -65536
VMEM_LIMIT = 48 * 1024 * 1024


def _params(*semantics):
    return pltpu.CompilerParams(dimension_semantics=semantics, vmem_limit_bytes=VMEM_LIMIT)


def _rms(x, g):
    return x * lax.rsqrt(jnp.mean(x * x, axis=-1, keepdims=True) + EPS) * g


def _softplus(z):
    return jnp.maximum(z, 0.0) + jnp.log1p(jnp.exp(-jnp.abs(z)))


def _resident(shape):
    return pl.BlockSpec(shape, lambda *_: (0,) * len(shape))


def _rglru_body(h_ref, g_ref, win_ref, cw_ref, cb_ref, wr_ref, wi_ref, br_ref, bi_ref, lam_ref,
                wout_ref, o_ref, tail_ref, state_ref):
    ts, _ = h_ref.shape
    width = wout_ref.shape[0]
    blk = width // LRU_HEADS

    @pl.when(pl.program_id(1) == 0)
    def _():
        tail_ref[...] = jnp.zeros_like(tail_ref)
        state_ref[...] = jnp.zeros_like(state_ref)

    x = h_ref[...]
    proj = jnp.dot(_rms(x, g_ref[...]).astype(BF16), win_ref[...], preferred_element_type=F32)
    yb, xb = proj[:, :width], proj[:, width:]

    row = lax.broadcasted_iota(jnp.int32, (ts, width), 0)
    ext = jnp.concatenate([tail_ref[...], xb], axis=0)
    xc = cw_ref[CONV_WIDTH - 1:CONV_WIDTH, :] * xb + cb_ref[...]
    for d in range(1, CONV_WIDTH):
        xc = xc + cw_ref[CONV_WIDTH - 1 - d:CONV_WIDTH - d, :] * pltpu.roll(ext, d, axis=0)[SUB:, :]
    tail_ref[...] = xb[ts - SUB:, :]

    xcb = xc.astype(BF16)
    r = jnp.concatenate([jnp.dot(xcb[:, k * blk:(k + 1) * blk], wr_ref[k], preferred_element_type=F32)
                         for k in range(LRU_HEADS)], axis=1)
    ig = jnp.concatenate([jnp.dot(xcb[:, k * blk:(k + 1) * blk], wi_ref[k], preferred_element_type=F32)
                          for k in range(LRU_HEADS)], axis=1)
    r = jax.nn.sigmoid(r + br_ref[...])
    ig = jax.nn.sigmoid(ig + bi_ref[...])
    log_a = -LRU_C * r * _softplus(-lam_ref[...])
    a = jnp.exp(log_a)
    u = jnp.sqrt(-jnp.tanh(log_a) * (a * a + 1.0)) * ig * xc

    d = 1
    while d < ts:
        keep = row >= d
        a_prev = jnp.where(keep, pltpu.roll(a, d, axis=0), 1.0)
        u_prev = jnp.where(keep, pltpu.roll(u, d, axis=0), 0.0)
        u = a * u_prev + u
        a = a * a_prev
        d *= 2
    hseq = u + a * state_ref[SUB - 1:SUB, :]
    state_ref[...] = hseq[ts - SUB:, :]

    gated = (jax.nn.gelu(yb) * hseq).astype(BF16)
    o_ref[...] = x + jnp.dot(gated, wout_ref[...], preferred_element_type=F32)


def rglru_mixer(h, g, w_in, conv_w, conv_b, w_r, w_i, b_r, b_i, lam, w_out, *, batch, ts=256):
    T, D = h.shape
    S = T // batch
    nt = S // ts
    W = w_out.shape[0]
    tile = pl.BlockSpec((ts, D), lambda b, j: (b * nt + j, 0))
    vec = lambda a: a.reshape(1, -1)
    return pl.pallas_call(
        _rglru_body,
        grid=(batch, nt),
        in_specs=[tile, _resident((1, D)), _resident((D, 2 * W)), _resident((CONV_WIDTH, W)),
                  _resident((1, W)), _resident(w_r.shape), _resident(w_i.shape), _resident((1, W)),
                  _resident((1, W)), _resident((1, W)), _resident((W, D))],
        out_specs=tile,
        out_shape=jax.ShapeDtypeStruct((T, D), F32),
        scratch_shapes=[pltpu.VMEM((SUB, W), F32), pltpu.VMEM((SUB, W), F32)],
        compiler_params=_params("parallel", "arbitrary"),
        name="rglru_mixer",
    )(h, vec(g), w_in.astype(BF16), conv_w, vec(conv_b), w_r.astype(BF16), w_i.astype(BF16),
      vec(b_r), vec(b_i), vec(lam), w_out.astype(BF16))


def _qkv_body(h_ref, gq_ref, gkv_ref, wq_ref, wkv_ref, q_ref, k_ref, v_ref, *, scale):
    x = h_ref[...]
    xhat = x * lax.rsqrt(jnp.mean(x * x, axis=-1, keepdims=True) + EPS)
    q = jnp.dot((xhat * gq_ref[...]).astype(BF16), wq_ref[...], preferred_element_type=F32)
    kv = jnp.dot((xhat * gkv_ref[...]).astype(BF16), wkv_ref[...], preferred_element_type=F32)
    aw = k_ref.shape[-1]
    q_ref[...] = (q * scale).astype(q_ref.dtype)
    k_ref[...] = kv[:, :aw].astype(k_ref.dtype)
    v_ref[...] = kv[:, aw:].astype(v_ref.dtype)


def qkv_project(h, g_q, g_kv, w_q, w_kv, *, tm=512):
    T, D = h.shape
    aw = w_q.shape[1]
    tile = pl.BlockSpec((tm, D), lambda i: (i, 0))
    out = pl.BlockSpec((tm, aw), lambda i: (i, 0))
    return pl.pallas_call(
        functools.partial(_qkv_body, scale=ATTN_HEAD_DIM ** -0.5),
        grid=(T // tm,),
        in_specs=[tile, _resident((1, D)), _resident((1, D)), _resident((D, aw)), _resident((D, 2 * aw))],
        out_specs=[out, out, out],
        out_shape=[jax.ShapeDtypeStruct((T, aw), BF16)] * 3,
        compiler_params=_params("parallel"),
        name="qkv_project",
    )(h, g_q.reshape(1, D), g_kv.reshape(1, D), w_q.astype(BF16), w_kv.astype(BF16))


ATTN_TQ, ATTN_TK = 512, 512


def _stick_body(q_ref, k_ref, v_ref, o_ref, *, tq, tk, dh):
    S, lanes = q_ref.shape
    nh = lanes // dh
    ndiag = tq // tk
    row = lax.broadcasted_iota(jnp.int32, (tk, tk), 0)
    col = lax.broadcasted_iota(jnp.int32, (tk, tk), 1)
    neg_from = jnp.where(row >= col, -1.0, 0.0).astype(BF16)
    qpos = lax.broadcasted_iota(jnp.int32, (tq, tk), 0)
    kpos = lax.broadcasted_iota(jnp.int32, (tq, tk), 1)
    head_of_lane = lax.broadcasted_iota(jnp.int32, (tq, lanes), 1) // dh

    def pair(q, k, v, carry, shift):
        acc, right = carry
        z = lax.dot_general(q, k, (((1,), (1,)), ((), ())), preferred_element_type=F32)
        sp = jnp.maximum(z, 0.0) + jnp.log(1.0 + jnp.exp(-jnp.abs(z)))
        if shift is not None:
            causal = kpos < qpos + shift
            sp = jnp.where(causal, sp, 0.0)
        hi = pltpu.bitcast(pltpu.bitcast(sp, jnp.int32) & HIGH_HALF, F32)
        lo = sp - hi
        tail = (jnp.dot(hi.astype(BF16), neg_from, preferred_element_type=F32)
                + jnp.dot(lo.astype(BF16), neg_from, preferred_element_type=F32))
        a = jnp.exp(z + tail + right)
        if shift is not None:
            a = jnp.where(causal, a, 0.0)
        acc = acc + jnp.dot(a.astype(BF16), v, preferred_element_type=F32)
        return acc, right + tail[:, 0:1]

    def pairs(qs, kb, carries, shift):
        k = k_ref[pl.ds(pl.multiple_of(kb * tk, tk), tk), :]
        v = v_ref[pl.ds(pl.multiple_of(kb * tk, tk), tk), :]
        return tuple(pair(qs[h], k, v, carries[h], shift) for h in range(nh))

    def q_block(qi, c):
        q = q_ref[pl.ds(pl.multiple_of(qi * tq, tq), tq), :]
        qs = [jnp.where(head_of_lane == h, q, jnp.zeros_like(q)) for h in range(nh)]
        carries = tuple((jnp.zeros((tq, lanes), F32), jnp.zeros((tq, 1), F32)) for _ in range(nh))
        last = (qi + 1) * ndiag - 1
        for d in range(ndiag):
            carries = pairs(qs, last - d, carries, (d + 1 - ndiag) * tk)

        def past(n, cr):
            for d in range(ndiag):
                cr = pairs(qs, (qi - n) * ndiag - 1 - d, cr, None)
            return cr

        carries = lax.fori_loop(0, qi, past, carries)
        out = carries[0][0]
        for h in range(1, nh):
            out = jnp.where(head_of_lane == h, carries[h][0], out)
        o_ref[pl.ds(pl.multiple_of(qi * tq, tq), tq), :] = out.astype(o_ref.dtype)
        return c

    lax.fori_loop(0, S // tq, q_block, 0)


def stick_breaking_attention(q, k, v, *, batch, tq=ATTN_TQ, tk=ATTN_TK):
    T, width = q.shape
    S = T // batch
    tq = min(tq, S)
    tk = min(tk, tq)
    spec = pl.BlockSpec((S, LANE), lambda b, g: (b, g))
    return pl.pallas_call(
        functools.partial(_stick_body, tq=tq, tk=tk, dh=ATTN_HEAD_DIM),
        grid=(batch, width // LANE),
        in_specs=[spec, spec, spec],
        out_specs=spec,
        out_shape=jax.ShapeDtypeStruct((T, width), BF16),
        compiler_params=_params("parallel", "parallel"),
        name="stick_breaking",
    )(q, k, v)


def _matmul_residual_body(h_ref, x_ref, w_ref, o_ref):
    o_ref[...] = h_ref[...] + jnp.dot(x_ref[...], w_ref[...], preferred_element_type=F32)


def matmul_residual(h, x, w, *, tm=512):
    T, D = h.shape
    K = x.shape[1]
    return pl.pallas_call(
        _matmul_residual_body,
        grid=(T // tm,),
        in_specs=[pl.BlockSpec((tm, D), lambda i: (i, 0)), pl.BlockSpec((tm, K), lambda i: (i, 0)),
                  _resident((K, D))],
        out_specs=pl.BlockSpec((tm, D), lambda i: (i, 0)),
        out_shape=jax.ShapeDtypeStruct((T, D), F32),
        compiler_params=_params("parallel"),
        name="matmul_residual",
    )(h, x, w.astype(BF16))


def _topk_rows(s, k, payload=None):
    n = s.shape[0]
    rows = lax.broadcasted_iota(jnp.int32, s.shape, 0)
    vals, sel = [], []
    for _ in range(k):
        best = jnp.max(s, axis=0, keepdims=True)
        pos = jnp.min(jnp.where(s == best, rows, n), axis=0, keepdims=True)
        hit = rows == pos
        vals.append(best)
        sel.append(pos if payload is None else jnp.max(jnp.where(hit, payload, -1), axis=0, keepdims=True))
        s = jnp.where(hit, -jnp.inf, s)
    return jnp.concatenate(vals, axis=0), jnp.concatenate(sel, axis=0)


def _peer_route_body(h_ref, g_ref, wqt_ref, keys_ref, eidx_ref, gates_ref):
    xn = _rms(h_ref[...], g_ref[...]).astype(BF16)
    qt = lax.dot_general(wqt_ref[...], xn, (((1,), (1,)), ((), ())), preferred_element_type=F32)
    half = keys_ref.shape[-1]
    k = PEER_TOPK
    for hd in range(PEER_HEADS):
        sv, si = [], []
        for part in range(2):
            hp = hd * 2 + part
            q = qt[hp * half:(hp + 1) * half, :].astype(BF16)
            scores = jnp.dot(keys_ref[hp], q, preferred_element_type=F32)
            v, i = _topk_rows(scores, k)
            sv.append(v)
            si.append(i)
        widths = [k // (a + 1) for a in range(k)]
        pad = -sum(widths) % SUB
        cand = jnp.concatenate([sv[0][a:a + 1, :] + sv[1][:w, :] for a, w in enumerate(widths)]
                               + [jnp.full((pad, sv[0].shape[1]), -jnp.inf, F32)], axis=0)
        pay = jnp.concatenate([si[0][a:a + 1, :] * PEER_NKEYS + si[1][:w, :] for a, w in enumerate(widths)]
                              + [jnp.zeros((pad, sv[0].shape[1]), jnp.int32)], axis=0)
        cv, ce = _topk_rows(cand, k, payload=pay)
        ex = jnp.exp(cv - cv[0:1, :])
        gates_ref[hd * k:(hd + 1) * k, :] = ex / jnp.sum(ex, axis=0, keepdims=True)
        eidx_ref[hd * k:(hd + 1) * k, :] = ce


def peer_route(h, g, w_q, sub_keys, *, tm=256):
    T, D = h.shape
    H, _, nkeys, half = sub_keys.shape
    npick = H * PEER_TOPK
    return pl.pallas_call(
        _peer_route_body,
        grid=(T // tm,),
        in_specs=[pl.BlockSpec((tm, D), lambda i: (i, 0)), _resident((1, D)),
                  _resident((H * 2 * half, D)), _resident((H * 2, nkeys, half))],
        out_specs=[pl.BlockSpec((npick, tm), lambda i: (0, i)),
                   pl.BlockSpec((npick, tm), lambda i: (0, i))],
        out_shape=[jax.ShapeDtypeStruct((npick, T), jnp.int32),
                   jax.ShapeDtypeStruct((npick, T), F32)],
        compiler_params=_params("parallel"),
        name="peer_route",
    )(h, g.reshape(1, D), w_q.T.astype(BF16), sub_keys.reshape(H * 2, nkeys, half).astype(BF16))


NPICK = PEER_HEADS * PEER_TOPK
NBUF = 3


def _expert_token(t, words_of_chunk, vs, h_ref, g_ref, gates_ref, o_ref, tok_lane):
    x = h_ref[pl.ds(t, 1), :]
    xn = _rms(x, g_ref[...])
    r = jnp.zeros((NPICK, LANE), F32)
    for c in range(SUB):
        words = words_of_chunk(c)
        u = pltpu.bitcast(words << 16, F32)
        vs[c] = pltpu.bitcast(words & HIGH_HALF, F32)
        r = r + u * xn[:, c * LANE:(c + 1) * LANE]
    dots = jnp.sum(r, axis=1, keepdims=True)
    gcol = jnp.sum(jnp.where(tok_lane == t, gates_ref[...], 0.0), axis=1, keepdims=True)
    coef = gcol * jax.nn.gelu(dots)
    ffn = jnp.concatenate(
        [jnp.sum(coef * vs[c], axis=0, keepdims=True) for c in range(SUB)], axis=1)
    o_ref[pl.ds(t, 1), :] = x + ffn


def _peer_expert_body(*refs, tt, nsteps, unroll, staged):
    if staged:
        (idx0, idx1, idx2, h_ref, g_ref, gates_ref, tab_ref, h2_ref, gates2_ref, rows_ref, o_ref, o2_ref,
         *scratch) = refs
    else:
        idx0, idx1, idx2, h_ref, g_ref, gates_ref, tab_ref, o_ref, *scratch = refs
    bufs, vscr, sem = scratch[:NBUF], scratch[NBUF], scratch[NBUF + 1]
    i = pl.program_id(0)
    tok_lane = lax.broadcasted_iota(jnp.int32, (NPICK, tt), 1)
    tok_lane2 = lax.broadcasted_iota(jnp.int32, (NPICK, tt // unroll), 1)

    def issue_token(idx_ref, t, p):
        for k in range(NPICK):
            row = pl.multiple_of((t * NPICK + k) * SUB, SUB)
            pltpu.make_async_copy(tab_ref.at[idx_ref[t, k]], bufs[p].at[pl.ds(row, SUB), :],
                                  sem.at[p]).start(priority=k % 2)

    def issue_step(idx_ref, p):
        def per_token(t, c):
            issue_token(idx_ref, t, p)
            return c
        lax.fori_loop(0, tt, per_token, 0)

    def compute_token(t, src, vs):
        base = pl.multiple_of(t * (NPICK * SUB), SUB)
        _expert_token(t, lambda c: src[pl.ds(base + c, NPICK, stride=SUB), :], vs,
                      h_ref, g_ref, gates_ref, o_ref, tok_lane)

    def compute_staged(t, vs):
        base = pl.multiple_of(t * NPICK, NPICK)
        _expert_token(t, lambda c: rows_ref[pl.ds(base, NPICK), c * LANE:(c + 1) * LANE], vs,
                      h2_ref, g_ref, gates2_ref, o2_ref, tok_lane2)

    @pl.when(i == 0)
    def _():
        issue_step(idx0, 0)
        if nsteps > 1:
            issue_step(idx1, 1)

    def step(p):
        src, ahead = bufs[p], (p + 2) % NBUF
        pltpu.make_async_copy(bufs[ahead], src, sem.at[p]).wait()

        def tokens(with_issue):
            def body(j, c):
                for r in range(unroll):
                    t = j * unroll + r
                    if with_issue:
                        issue_token(idx2, t, ahead)
                    compute_token(t, src, vscr.at[r])
                if staged:
                    compute_staged(j, vscr.at[unroll])
                return c
            lax.fori_loop(0, tt // unroll, body, 0)

        @pl.when(i + 2 < nsteps)
        def _():
            tokens(True)

        @pl.when(i + 2 >= nsteps)
        def _():
            tokens(False)

    for p in range(NBUF):
        @pl.when(i % NBUF == p)
        def _(p=p):
            step(p)


def pack_uv(u, v):
    E, D = u.shape
    ub = lax.bitcast_convert_type(u.astype(BF16), jnp.uint16).astype(jnp.uint32)
    vb = lax.bitcast_convert_type(v.astype(BF16), jnp.uint16).astype(jnp.uint32)
    return lax.bitcast_convert_type(ub | (vb << 16), jnp.int32).reshape(E, SUB, D // SUB)


def peer_expert(h, g, eidx, gates_t, tab, staged=None, *, tt=16, unroll=2):
    T, D = h.shape
    nsteps = T // tt
    ts = tt // unroll
    tile = pl.BlockSpec((tt, D), lambda i: (i, 0))
    in_specs = [
        pl.BlockSpec((tt, NPICK), lambda i: (i, 0), memory_space=pltpu.SMEM),
        pl.BlockSpec((tt, NPICK), lambda i: (jnp.minimum(i + 1, nsteps - 1), 0), memory_space=pltpu.SMEM),
        pl.BlockSpec((tt, NPICK), lambda i: (jnp.minimum(i + 2, nsteps - 1), 0), memory_space=pltpu.SMEM),
        tile, _resident((1, D)), pl.BlockSpec((None, NPICK, tt), lambda i: (i, 0, 0)),
        pl.BlockSpec(memory_space=pl.ANY)]
    args = [eidx, eidx, eidx, h, g.reshape(1, D), gates_t, tab]
    out_specs, out_shape = tile, jax.ShapeDtypeStruct((T, D), F32)
    if staged is not None:
        tile2 = pl.BlockSpec((ts, D), lambda i: (i, 0))
        in_specs += [tile2, pl.BlockSpec((None, NPICK, ts), lambda i: (i, 0, 0)),
                     pl.BlockSpec((ts * NPICK, D), lambda i: (i, 0))]
        args += list(staged)
        out_specs, out_shape = [tile, tile2], [out_shape, jax.ShapeDtypeStruct((T // unroll, D), F32)]
    return pl.pallas_call(
        functools.partial(_peer_expert_body, tt=tt, nsteps=nsteps, unroll=unroll, staged=staged is not None),
        grid=(nsteps,),
        in_specs=in_specs,
        out_specs=out_specs,
        out_shape=out_shape,
        scratch_shapes=[
            *[pltpu.VMEM((tt * NPICK * SUB, LANE), jnp.int32)] * NBUF,
            pltpu.VMEM((unroll + 1, SUB, NPICK, LANE), F32),
            pltpu.SemaphoreType.DMA((NBUF,)),
        ],
        compiler_params=_params("arbitrary"),
        name="peer_expert_mixed" if staged is not None else "peer_expert",
    )(*args)


SC_ROWS = 32
SC_IDX = 128
SC_WORKERS = 20


def sc_gather_rows(tab, idx):
    _, W = tab.shape
    N = idx.shape[0]
    info = plsc.get_sparse_core_info()
    assert SC_WORKERS <= info.num_cores * info.num_subcores
    per_w = N // SC_WORKERS
    assert per_w * SC_WORKERS == N and per_w % SC_IDX == 0
    mesh = plsc.VectorSubcoreMesh(core_axis_name="core", subcore_axis_name="subcore")

    @functools.partial(
        pl.kernel, out_type=jax.ShapeDtypeStruct((N, W), tab.dtype), mesh=mesh,
        scratch_types=[pltpu.VMEM((SC_IDX,), jnp.int32), pltpu.VMEM((SC_ROWS, W), tab.dtype),
                       pltpu.VMEM((SC_ROWS, W), tab.dtype), pltpu.SemaphoreType.DMA, pltpu.SemaphoreType.DMA,
                       pltpu.SemaphoreType.DMA])
    def gather(tab_hbm, idx_hbm, out_hbm, idx_v, rows0, rows1, gsem, wsem0, wsem1):
        wid = lax.axis_index("subcore") * info.num_cores + lax.axis_index("core")
        base = wid * per_w
        bufs = ((rows0, wsem0), (rows1, wsem1))

        @pl.when(wid < SC_WORKERS)
        def _():
            @pl.loop(0, per_w // SC_IDX)
            def _(c):
                off = base + c * SC_IDX
                pltpu.sync_copy(idx_hbm.at[pl.ds(off, SC_IDX)], idx_v)
                for j in range(SC_IDX // SC_ROWS):
                    buf, wsem = bufs[j % 2]
                    dst = out_hbm.at[pl.ds(off + j * SC_ROWS, SC_ROWS)]

                    @pl.when((c > 0) | (j >= 2))
                    def _():
                        pltpu.make_async_copy(buf, dst, wsem).wait()

                    pltpu.async_copy(tab_hbm.at[idx_v.at[pl.ds(j * SC_ROWS, SC_ROWS)]], buf, gsem).wait()
                    pltpu.async_copy(buf, dst, wsem)

            for buf, wsem in bufs:
                pltpu.make_async_copy(buf, out_hbm.at[pl.ds(base, SC_ROWS)], wsem).wait()

    return gather(tab, idx)


PEER_LEAD = 2768
PEER_CALLS = 5


def _tile_gates(gates_t, tt):
    npick, T = gates_t.shape
    return gates_t.reshape(npick, T // tt, tt).transpose(1, 0, 2)


def peer_ffn(h, g, w_q, sub_keys, u_tab, v_tab, *, tt=16, unroll=2):
    T, D = h.shape
    eidx_t, gates_t = peer_route(h, g, w_q, sub_keys)
    eidx = eidx_t.T
    tab = pack_uv(u_tab, v_tab)
    tab2d = tab.reshape(tab.shape[0], D)
    part = (T - PEER_LEAD) // ((unroll + 1) * PEER_CALLS)
    assert PEER_LEAD + (unroll + 1) * PEER_CALLS * part == T and part % tt == 0 and PEER_LEAD % tt == 0
    gates = lambda lo, hi, tile: _tile_gates(gates_t[:, lo:hi], tile)
    outs = [peer_expert(h[:PEER_LEAD], g, eidx[:PEER_LEAD], gates(0, PEER_LEAD, tt), tab, tt=tt, unroll=unroll)]
    for c in range(PEER_CALLS):
        lo = PEER_LEAD + (unroll + 1) * c * part
        mid, hi = lo + unroll * part, lo + (unroll + 1) * part
        rows = sc_gather_rows(tab2d, eidx[mid:hi].reshape(-1))
        outs += peer_expert(h[lo:mid], g, eidx[lo:mid], gates(lo, mid, tt), tab,
                            staged=(h[mid:hi], gates(mid, hi, tt // unroll), rows), tt=tt, unroll=unroll)
    return jnp.concatenate(outs, axis=0)


def _ple_body(h_ref, p_ref, g_ref, wg_ref, wp_ref, gf_ref, o_ref, *, final):
    x = h_ref[...]
    gate = jax.nn.sigmoid(jnp.dot(_rms(x, g_ref[...]).astype(BF16), wg_ref[...], preferred_element_type=F32))
    emb = jnp.dot(p_ref[...].astype(BF16), wp_ref[...], preferred_element_type=F32)
    y = x + gate * emb
    o_ref[...] = _rms(y, gf_ref[...]) if final else y


def per_layer_embed(h, p, g, w_gate, w_proj, g_final, *, final, tm=512):
    T, D = h.shape
    P = p.shape[1]
    tile = pl.BlockSpec((tm, D), lambda i: (i, 0))
    return pl.pallas_call(
        functools.partial(_ple_body, final=final),
        grid=(T // tm,),
        in_specs=[tile, pl.BlockSpec((tm, P), lambda i: (i, 0)), _resident((1, D)), _resident((D, D)),
                  _resident((P, D)), _resident((1, D))],
        out_specs=tile,
        out_shape=jax.ShapeDtypeStruct((T, D), F32),
        compiler_params=_params("parallel"),
        name="per_layer_embed",
    )(h, p, g.reshape(1, D), w_gate.astype(BF16), w_proj.astype(BF16), g_final.reshape(1, D))


def kernel(x, p, norm_mix, a_w_in, a_conv_w, a_conv_b, a_w_r, a_w_i, a_b_r, a_b_i, a_lambda,
           a_w_out, kv_norm, w_kv, b_w_q, b_w_o, norm_ffn, peer_w_q, peer_sub_keys, peer_u,
           peer_v, norm_ple, ple_w_gate, ple_w_proj, final_norm):
    B, S, D = x.shape
    T = B * S
    depth = norm_mix.shape[0]
    n_a = a_w_in.shape[0]
    h = x.reshape(T, D)
    k_sh = v_sh = None
    for i in range(depth):
        if i < n_a:
            j = i
            h = rglru_mixer(h, norm_mix[i], a_w_in[j], a_conv_w[j], a_conv_b[j], a_w_r[j], a_w_i[j],
                            a_b_r[j], a_b_i[j], a_lambda[j], a_w_out[j], batch=B)
        else:
            j = i - n_a
            q, k_new, v_new = qkv_project(h, norm_mix[i], kv_norm, b_w_q[j], w_kv)
            if i == n_a:
                k_sh, v_sh = k_new, v_new
            o = stick_breaking_attention(q, k_sh, v_sh, batch=B)
            h = matmul_residual(h, o, b_w_o[j])
        h = peer_ffn(h, norm_ffn[i], peer_w_q[i], peer_sub_keys[i], peer_u[i], peer_v[i])
        h = per_layer_embed(h, p[i].reshape(T, -1), norm_ple[i], ple_w_gate[i], ple_w_proj[i], final_norm,
                            final=(i == depth - 1))
    return h.reshape(B, S, D)
```
